```python
import jax, jax.numpy as jnp
from jax import lax
import numpy as np

D_MODEL = 1024
BATCH = 8
SEQ = 2048
DEPTH = 1
DEC_BATCH = 4
DEC_SEQ = 8192
PAST_LEN = 128

EPS = 1e-6
GLA_HEADS = 4
GLA_DV = 128
GLA_DK = 64
GLA_WIDTH = GLA_HEADS * GLA_DV
GLA_KEY_WIDTH = GLA_HEADS * GLA_DK
GATE_RANK = 16
GATE_NORMALIZER = 16.0
CHUNK = 64
F_GROUPS = 4
F_GROUP_DIM = 128
F_WIDTH = F_GROUPS * F_GROUP_DIM
MIX_WIDTH = GLA_WIDTH + F_WIDTH
IN_COLS = (GLA_KEY_WIDTH, GLA_KEY_WIDTH, GLA_WIDTH, GLA_WIDTH, GATE_RANK, GATE_RANK, F_WIDTH)
IN_WIDTH = sum(IN_COLS)
IN_SPLITS = list(np.cumsum(IN_COLS)[:-1])
N_GROUPS = 4
EXPERTS_PER_GROUP = 8
N_EXPERTS = N_GROUPS * EXPERTS_PER_GROUP
TOP_K = 2
D_EXPERT = 256

kernel_name = "hymba_gla_fnet_hiermoe_encoder"


def rmsnorm(x, g):
    xf = x.astype(jnp.float32)
    y = xf * lax.rsqrt(jnp.mean(xf * xf, axis=-1, keepdims=True) + EPS)
    return (y * g.astype(jnp.float32)).astype(x.dtype)


def gla_chunked(q, k, v, log_a):
    B, S, H, _ = q.shape
    N = S // CHUNK
    rs = lambda t: t.reshape(B, N, CHUNK, H, t.shape[-1])
    q, k, v, log_a = rs(q), rs(k), rs(v), rs(log_a)
    b = jnp.cumsum(log_a, axis=2)
    b_last = b[:, :, -1:]
    q_dec = q * jnp.exp(b)
    k_dec = k * jnp.exp(-b)
    k_tail = k * jnp.exp(b_last - b)
    mask = jnp.tril(jnp.ones((CHUNK, CHUNK), dtype=bool))
    att = jnp.einsum('bnihk,bnjhk->bnhij', q_dec, k_dec)
    att = jnp.where(mask, att, 0.0)
    o_intra = jnp.einsum('bnhij,bnjhv->bnihv', att, v)
    kv = jnp.einsum('bnjhk,bnjhv->bnhkv', k_tail, v)
    decay = jnp.exp(b_last[:, :, 0])

    def step(state, inp):
        d, u = inp
        return state * d[..., None] + u, state

    init = jnp.zeros((B, H, q.shape[-1], v.shape[-1]), jnp.float32)
    _, s_prev = lax.scan(step, init, (jnp.swapaxes(decay, 0, 1), jnp.swapaxes(kv, 0, 1)))
    s_prev = jnp.swapaxes(s_prev, 0, 1)
    o_inter = jnp.einsum('bnihk,bnhkv->bnihv', q_dec, s_prev)
    return (o_intra + o_inter).reshape(B, S, H, v.shape[-1])


def token_mixer(xn, w_in, w_gk2_f, b_gk_f, w_gk2_b, b_gk_b, gla_norm_g, w_out):
    B, S, _ = xn.shape
    f32 = lambda t: t.astype(jnp.float32)
    proj = xn @ w_in
    q, k, v, r, g_f, g_b, fx = jnp.split(proj, IN_SPLITS, axis=-1)
    q = f32(q).reshape(B, S, GLA_HEADS, GLA_DK) * (GLA_DK ** -0.5)
    k = f32(k).reshape(B, S, GLA_HEADS, GLA_DK)
    v = f32(v).reshape(B, S, GLA_HEADS, GLA_DV)
    la_f = (jax.nn.log_sigmoid(f32(g_f) @ f32(w_gk2_f) + f32(b_gk_f)) / GATE_NORMALIZER).reshape(B, S, GLA_HEADS, GLA_DK)
    la_b = (jax.nn.log_sigmoid(f32(g_b) @ f32(w_gk2_b) + f32(b_gk_b)) / GATE_NORMALIZER).reshape(B, S, GLA_HEADS, GLA_DK)
    o_fwd = gla_chunked(q, k, v, la_f)
    flip = lambda t: jnp.flip(t, axis=1)
    o_bwd = flip(gla_chunked(flip(q), flip(k), flip(v), flip(la_b)))
    o = o_fwd + o_bwd
    o = o * lax.rsqrt(jnp.mean(o * o, axis=-1, keepdims=True) + EPS)
    o = o * f32(gla_norm_g) * jax.nn.silu(f32(r).reshape(B, S, GLA_HEADS, GLA_DV))
    o = o.reshape(B, S, GLA_WIDTH)
    u = f32(fx).reshape(B, S, F_GROUPS, F_GROUP_DIM)
    fo = jnp.real(jnp.fft.fft2(u, axes=(1, 3), norm="ortho")).reshape(B, S, F_WIDTH)
    mix = jnp.concatenate([o, fo], axis=-1).astype(xn.dtype)
    return mix @ w_out


def hier_moe(xn, w_group, w_expert, w_gate, w_up, w_down):
    B, S, D = xn.shape
    x2 = xn.reshape(B * S, D)
    T = x2.shape[0]
    p_group = jax.nn.softmax((x2 @ w_group).astype(jnp.float32), axis=-1)
    g_sel = jnp.argmax(p_group, axis=-1)
    g_w = jnp.take_along_axis(p_group, g_sel[:, None], axis=-1)
    logits_e = (x2 @ w_expert).astype(jnp.float32).reshape(T, N_GROUPS, EXPERTS_PER_GROUP)
    sel_logits = jnp.take_along_axis(logits_e, g_sel[:, None, None], axis=1)[:, 0]
    p_e = jax.nn.softmax(sel_logits, axis=-1)
    top_w, top_i = lax.top_k(p_e, TOP_K)
    top_w = top_w / jnp.sum(top_w, axis=-1, keepdims=True)
    w_e = jnp.sum(jax.nn.one_hot(top_i, EXPERTS_PER_GROUP, dtype=jnp.float32) * top_w[..., None], axis=1)
    combine = jax.nn.one_hot(g_sel, N_GROUPS, dtype=jnp.float32)[:, :, None] * (g_w * w_e)[:, None, :]
    out = jnp.zeros((T, D), jnp.float32)
    for g in range(N_GROUPS):
        sl = slice(g * EXPERTS_PER_GROUP, (g + 1) * EXPERTS_PER_GROUP)
        h = jax.nn.silu(jnp.einsum('td,edf->tef', x2, w_gate[sl])) * jnp.einsum('td,edf->tef', x2, w_up[sl])
        h = h * combine[:, g, :, None].astype(h.dtype)
        out = out + jnp.einsum('tef,efd->td', h, w_down[sl]).astype(jnp.float32)
    return out.astype(xn.dtype).reshape(B, S, D)


def encoder(x, norm1_g, w_in, w_gk2_f, b_gk_f, w_gk2_b, b_gk_b, gla_norm_g, w_out,
            norm2_g, w_group, w_expert, w_gate, w_up, w_down, norm_f_g):
    h = x
    for l in range(DEPTH):
        h = h + token_mixer(rmsnorm(h, norm1_g[l]), w_in[l], w_gk2_f[l], b_gk_f[l],
                            w_gk2_b[l], b_gk_b[l], gla_norm_g[l], w_out[l])
        h = h + hier_moe(rmsnorm(h, norm2_g[l]), w_group[l], w_expert[l],
                         w_gate[l], w_up[l], w_down[l])
    return rmsnorm(h, norm_f_g)


def setup_inputs(seed: int = 0) -> dict:
    key = jax.random.key(seed)
    ks = jax.random.split(key, 20)
    n = lambda k, shape, scale: jax.random.normal(k, shape, jnp.float32) * scale
    L, D = DEPTH, D_MODEL
    return {
        "x_prompt": n(ks[0], (BATCH, SEQ, D), 1.0),
        "x_sample": n(ks[1], (DEC_BATCH, DEC_SEQ, D), 1.0),
        "norm1_g": 1.0 + n(ks[2], (L, D), 0.02),
        "w_in": n(ks[3], (L, D, IN_WIDTH), D ** -0.5),
        "w_gk2_f": n(ks[4], (L, GATE_RANK, GLA_KEY_WIDTH), GATE_RANK ** -0.5),
        "b_gk_f": n(ks[5], (L, GLA_KEY_WIDTH), 0.1),
        "w_gk2_b": n(ks[6], (L, GATE_RANK, GLA_KEY_WIDTH), GATE_RANK ** -0.5),
        "b_gk_b": n(ks[7], (L, GLA_KEY_WIDTH), 0.1),
        "gla_norm_g": 1.0 + n(ks[8], (L, GLA_DV), 0.02),
        "w_out": n(ks[9], (L, MIX_WIDTH, D), MIX_WIDTH ** -0.5),
        "norm2_g": 1.0 + n(ks[10], (L, D), 0.02),
        "w_group": n(ks[11], (L, D, N_GROUPS), D ** -0.5),
        "w_expert": n(ks[12], (L, D, N_EXPERTS), D ** -0.5),
        "w_gate": n(ks[13], (L, N_EXPERTS, D, D_EXPERT), D ** -0.5),
        "w_up": n(ks[14], (L, N_EXPERTS, D, D_EXPERT), D ** -0.5),
        "w_down": n(ks[15], (L, N_EXPERTS, D_EXPERT, D), D_EXPERT ** -0.5),
        "norm_f_g": 1.0 + n(ks[16], (D,), 0.02),
    }


def reference(x_prompt, x_sample, norm1_g, w_in, w_gk2_f, b_gk_f, w_gk2_b, b_gk_b,
              gla_norm_g, w_out, norm2_g, w_group, w_expert, w_gate, w_up, w_down, norm_f_g):
    y_prompt = encoder(x_prompt, norm1_g, w_in, w_gk2_f, b_gk_f, w_gk2_b, b_gk_b, gla_norm_g,
                       w_out, norm2_g, w_group, w_expert, w_gate, w_up, w_down, norm_f_g)
    y_sample = encoder(x_sample, norm1_g, w_in, w_gk2_f, b_gk_f, w_gk2_b, b_gk_b, gla_norm_g,
                       w_out, norm2_g, w_group, w_expert, w_gate, w_up, w_down, norm_f_g)
    return (y_prompt, y_sample)
```

```python
import functools

import numpy as np
import jax
import jax.numpy as jnp
from jax import lax
from jax.experimental import pallas as pl
from jax.experimental.pallas import tpu as pltpu

D_MODEL = 1024
EPS = 1e-6
GLA_HEADS = 4
GLA_DV = 128
GLA_DK = 64
GLA_WIDTH = GLA_HEADS * GLA_DV
GLA_KEY_WIDTH = GLA_HEADS * GLA_DK
GATE_RANK = 16
GATE_NORMALIZER = 16.0
CHUNK = 64
F_GROUPS = 4
F_GROUP_DIM = 128
F_WIDTH = F_GROUPS * F_GROUP_DIM
N_GROUPS = 4
EXPERTS_PER_GROUP = 8
N_EXPERTS = N_GROUPS * EXPERTS_PER_GROUP
D_EXPERT = 256

LANES = 128
V7X_VMEM_BYTES = 64 * 1024 * 1024
VMEM_LIMIT = 56 * 1024 * 1024

TM_TOK = 512
SUB = 128
DFT_N = 2048
DFT_ROWS = 512
TM_EXP = 256
TM_DISP = 512
TM_COMB = 256
ROUTER_ROWS = 48
GATE_COLS = 128
IN_COLS_PAD = 2048 + GATE_COLS
TOK_ROWS = D_MODEL // LANES
N_FILL = 2 * N_EXPERTS

BF16 = jnp.bfloat16
F32 = jnp.float32


def _dot(a, b):
    return jnp.dot(a, b, preferred_element_type=F32)


def _dot_nt(a, b):
    return lax.dot_general(a, b, (((1,), (1,)), ((), ())), preferred_element_type=F32)


def _dot_tn(a, b):
    return lax.dot_general(a, b, (((0,), (0,)), ((), ())), preferred_element_type=F32)


def _tri_tables():
    r = np.arange(SUB)
    same = (r[:, None] // CHUNK) == (r[None, :] // CHUNK)
    l_incl = same & (r[None, :] <= r[:, None])
    u_strict = same & (r[None, :] > r[:, None])
    u_incl = same & (r[None, :] >= r[:, None])
    l_strict = same & (r[None, :] < r[:, None])
    fwd = np.concatenate([l_incl, u_strict], 0).astype(np.float32)
    bwd = np.concatenate([u_incl, l_strict], 0).astype(np.float32)
    return fwd, bwd


def _chan_dft_table():
    c = np.arange(F_GROUP_DIM)
    ang = 2.0 * np.pi * ((c[:, None] * c[None, :]) % F_GROUP_DIM) / F_GROUP_DIM
    s = 1.0 / np.sqrt(F_GROUP_DIM)
    return np.concatenate([np.cos(ang) * s, -np.sin(ang) * s], 1).astype(np.float32)


def _seq_dft_table():
    k = np.arange(DFT_N, dtype=np.int64)
    ang = 2.0 * np.pi * ((k[:, None] * k[None, :]) % DFT_N) / DFT_N
    return np.concatenate([np.cos(ang), np.sin(ang)], 1).astype(np.float32)


def _twiddle_tables(radix):
    k1 = np.arange(radix, dtype=np.int64)[:, None]
    s2 = np.arange(DFT_N, dtype=np.int64)[None, :]
    ang = 2.0 * np.pi * ((k1 * s2) % (radix * DFT_N)) / (radix * DFT_N)
    c = np.repeat(np.cos(ang)[:, :, None], LANES, 2).astype(np.float32)
    s = np.repeat(np.sin(ang)[:, :, None], LANES, 2).astype(np.float32)
    return c, s


_TRI_F, _TRI_B = _tri_tables()
_CHAN_DFT = _chan_dft_table()
_SEQ_DFT = _seq_dft_table()


def _inproj_body(x_ref, g1_ref, w_ref, wg_ref, bg_ref, trif_ref, trib_ref, cs_ref,
                 qf_ref, kf_ref, tf_ref, qb_ref, kb_ref, tb_ref, v_ref, r_ref, u_ref,
                 decf_ref, decb_ref, tot_ref):
    x = x_ref[...]
    ms = jnp.mean(x * x, axis=-1, keepdims=True)
    xn = (x * lax.rsqrt(ms + EPS) * g1_ref[...]).astype(BF16)
    qk = _dot(xn, w_ref[:, 0:2 * GLA_KEY_WIDTH])
    v_ref[...] = _dot(xn, w_ref[:, 512:1024]).astype(BF16)
    r_ref[...] = _dot(xn, w_ref[:, 1024:1536]).astype(BF16)
    fx = _dot(xn, w_ref[:, 1536:2048]).astype(BF16)
    gates = _dot(xn, w_ref[:, 2048:IN_COLS_PAD]).astype(BF16)
    z = _dot(gates, wg_ref[...]) + bg_ref[...]
    la = (jnp.minimum(z, 0.0) - jnp.log1p(jnp.exp(-jnp.abs(z)))) * (1.0 / GATE_NORMALIZER)
    la_hi = la.astype(BF16)
    la_lo = (la - la_hi.astype(F32)).astype(BF16)
    trif = trif_ref[...]
    trib = trib_ref[...]
    scale = GLA_DK ** -0.5
    for s in range(TM_TOK // SUB):
        rows = slice(s * SUB, (s + 1) * SUB)
        q = qk[rows, 0:GLA_KEY_WIDTH]
        k = qk[rows, GLA_KEY_WIDTH:2 * GLA_KEY_WIDTH]
        rf = _dot(trif, la_hi[rows, 0:GLA_KEY_WIDTH]) + _dot(trif, la_lo[rows, 0:GLA_KEY_WIDTH])
        b, tl = rf[0:SUB], rf[SUB:2 * SUB]
        qf_ref[rows, :] = (q * scale * jnp.exp(b)).astype(BF16)
        kf_ref[rows, :] = (k * jnp.exp(-b)).astype(BF16)
        tf_ref[rows, :] = (k * jnp.exp(tl)).astype(BF16)
        totf = b + tl
        tot_ref[0, rows, :] = totf[:, 0:LANES]
        tot_ref[1, rows, :] = totf[:, LANES:]
        rb = _dot(trib, la_hi[rows, GLA_KEY_WIDTH:]) + _dot(trib, la_lo[rows, GLA_KEY_WIDTH:])
        c, tlb = rb[0:SUB], rb[SUB:2 * SUB]
        qb_ref[rows, :] = (q * scale * jnp.exp(c)).astype(BF16)
        kb_ref[rows, :] = (k * jnp.exp(-c)).astype(BF16)
        tb_ref[rows, :] = (k * jnp.exp(tlb)).astype(BF16)
        totb = c + tlb
        tot_ref[2, rows, :] = totb[:, 0:LANES]
        tot_ref[3, rows, :] = totb[:, LANES:]
    chunk_rows = pl.ds(0, TM_TOK // CHUNK, stride=CHUNK)
    decf_ref[:, 0:LANES] = jnp.exp(tot_ref[0, chunk_rows, :])
    decf_ref[:, LANES:] = jnp.exp(tot_ref[1, chunk_rows, :])
    decb_ref[:, 0:LANES] = jnp.exp(tot_ref[2, chunk_rows, :])
    decb_ref[:, LANES:] = jnp.exp(tot_ref[3, chunk_rows, :])
    cs = cs_ref[...]
    for g in range(F_GROUPS):
        res = _dot(fx[:, g * LANES:(g + 1) * LANES], cs)
        u_ref[:, g * LANES:(g + 1) * LANES] = res[:, 0:LANES].astype(BF16)
        u_ref[:, F_WIDTH + g * LANES:F_WIDTH + (g + 1) * LANES] = res[:, LANES:].astype(BF16)


def _inproj(x2, g1, w_in_r, wg, bg):
    t = x2.shape[0]
    nt = t // TM_TOK
    tok = lambda w: pl.BlockSpec((TM_TOK, w), lambda i: (i, 0))
    full = lambda a: pl.BlockSpec(a.shape, lambda i: (0,) * a.ndim)
    trif = jnp.asarray(_TRI_F, BF16)
    trib = jnp.asarray(_TRI_B, BF16)
    cs = jnp.asarray(_CHAN_DFT, BF16)
    kw = jax.ShapeDtypeStruct((t, GLA_KEY_WIDTH), BF16)
    dec = jax.ShapeDtypeStruct((t // CHUNK, GLA_KEY_WIDTH), F32)
    dec_spec = pl.BlockSpec((TM_TOK // CHUNK, GLA_KEY_WIDTH), lambda i: (i, 0))
    return pl.pallas_call(
        _inproj_body,
        grid=(nt,),
        in_specs=[tok(D_MODEL), full(g1), full(w_in_r), full(wg), full(bg), full(trif), full(trib), full(cs)],
        out_specs=[tok(GLA_KEY_WIDTH)] * 6 + [tok(GLA_WIDTH), tok(GLA_WIDTH), tok(2 * F_WIDTH), dec_spec, dec_spec],
        out_shape=[kw] * 6 + [jax.ShapeDtypeStruct((t, GLA_WIDTH), BF16), jax.ShapeDtypeStruct((t, GLA_WIDTH), BF16),
                              jax.ShapeDtypeStruct((t, 2 * F_WIDTH), BF16), dec, dec],
        scratch_shapes=[pltpu.VMEM((2 * GLA_KEY_WIDTH // LANES, TM_TOK, LANES), F32)],
        compiler_params=pltpu.CompilerParams(dimension_semantics=("arbitrary",), vmem_limit_bytes=VMEM_LIMIT),
        name="inproj",
    )(x2, g1, w_in_r, wg, bg, trif, trib, cs)


def _gla_chunk(q_ref, k_ref, t_ref, v_ref, d_ref, o_ref, s_ref, c, causal, m_lo, mv_lo, bd):
    rows = slice(c * CHUNK, (c + 1) * CHUNK)
    dec = d_ref[c:c + 1, :]
    for p in range(GLA_HEADS // 2):
        kl = slice(p * LANES, (p + 1) * LANES)
        vl = slice(p * 2 * GLA_DV, (p + 1) * 2 * GLA_DV)
        qd = q_ref[rows, kl]
        kd = k_ref[rows, kl]
        kt = t_ref[rows, kl]
        vv = v_ref[rows, vl]
        zk = jnp.zeros_like(kd)
        zv = jnp.zeros_like(vv)
        kbd = jnp.concatenate([jnp.where(m_lo, kd, zk), jnp.where(m_lo, zk, kd)], axis=0)
        att = _dot_nt(qd, kbd)
        att = jnp.where(causal, att, 0.0).astype(BF16)
        vbd = jnp.concatenate([jnp.where(mv_lo, vv, zv), jnp.where(mv_lo, zv, vv)], axis=0)
        st = s_ref[p]
        o_ref[rows, vl] = _dot(att, vbd) + _dot_nt(qd, st.astype(BF16))
        kv = _dot_tn(vv, kt)
        s_ref[p] = st * dec[:, kl] + jnp.where(bd, kv, 0.0)


def _gla_body(qf_ref, kf_ref, tf_ref, vf_ref, df_ref, qb_ref, kb_ref, tb_ref, vb_ref, db_ref,
              of_ref, ob_ref, sf_ref, sb_ref):
    @pl.when(pl.program_id(1) == 0)
    def _():
        sf_ref[...] = jnp.zeros_like(sf_ref)
        sb_ref[...] = jnp.zeros_like(sb_ref)

    lane = lax.broadcasted_iota(jnp.int32, (CHUNK, LANES), 1)
    row = lax.broadcasted_iota(jnp.int32, (CHUNK, LANES), 0)
    m_lo = lane < GLA_DK
    col = lane & (CHUNK - 1)
    causal_f = row >= col
    causal_b = row <= col
    mv_lo = lax.broadcasted_iota(jnp.int32, (CHUNK, 2 * GLA_DV), 1) < GLA_DV
    bd = ((lax.broadcasted_iota(jnp.int32, (2 * GLA_DV, LANES), 0) < GLA_DV)
          == (lax.broadcasted_iota(jnp.int32, (2 * GLA_DV, LANES), 1) < GLA_DK))
    n = TM_TOK // CHUNK
    for j in range(n):
        _gla_chunk(qf_ref, kf_ref, tf_ref, vf_ref, df_ref, of_ref, sf_ref, j, causal_f, m_lo, mv_lo, bd)
        _gla_chunk(qb_ref, kb_ref, tb_ref, vb_ref, db_ref, ob_ref, sb_ref, n - 1 - j, causal_b, m_lo, mv_lo, bd)


def _gla(qf, kf, tf, qb, kb, tb, v, decf, decb, batch, seq):
    t = batch * seq
    nt = seq // TM_TOK
    fwd = lambda b, i: (b * nt + i, 0)
    bwd = lambda b, i: (b * nt + nt - 1 - i, 0)
    ks = lambda m: pl.BlockSpec((TM_TOK, GLA_KEY_WIDTH), m)
    vs = lambda m: pl.BlockSpec((TM_TOK, GLA_WIDTH), m)
    ds = lambda m: pl.BlockSpec((TM_TOK // CHUNK, GLA_KEY_WIDTH), m)
    o = jax.ShapeDtypeStruct((t, GLA_WIDTH), F32)
    state = pltpu.VMEM((GLA_HEADS // 2, 2 * GLA_DV, LANES), F32)
    return pl.pallas_call(
        _gla_body,
        grid=(batch, nt),
        in_specs=[ks(fwd), ks(fwd), ks(fwd), vs(fwd), ds(fwd), ks(bwd), ks(bwd), ks(bwd), vs(bwd), ds(bwd)],
        out_specs=[vs(fwd), vs(bwd)],
        out_shape=[o, o],
        scratch_shapes=[state, state],
        compiler_params=pltpu.CompilerParams(dimension_semantics=("arbitrary", "arbitrary"),
                                             vmem_limit_bytes=VMEM_LIMIT),
        name="gla",
    )(qf, kf, tf, v, decf, qb, kb, tb, v, decb)


RADIX_ROWS = 256


def _radix4_body(z_ref, twc_ref, tws_ref, y_ref):
    z = [z_ref[s].astype(F32) for s in range(4)]
    re = [a[:, 0:F_WIDTH] for a in z]
    im = [a[:, F_WIDTH:] for a in z]
    ar, ai = re[0] + re[2], im[0] + im[2]
    br, bi = re[0] - re[2], im[0] - im[2]
    cr, ci = re[1] + re[3], im[1] + im[3]
    dr, di = re[1] - re[3], im[1] - im[3]
    y = [(ar + cr, ai + ci), (br + di, bi - dr), (ar - cr, ai - ci), (br - di, bi + dr)]
    y_ref[0, :, 0:F_WIDTH] = y[0][0].astype(BF16)
    y_ref[0, :, F_WIDTH:] = y[0][1].astype(BF16)
    for k1 in range(1, 4):
        c = jnp.concatenate([twc_ref[k1]] * (F_WIDTH // LANES), axis=1)
        s = jnp.concatenate([tws_ref[k1]] * (F_WIDTH // LANES), axis=1)
        yr, yi = y[k1]
        y_ref[k1, :, 0:F_WIDTH] = (yr * c + yi * s).astype(BF16)
        y_ref[k1, :, F_WIDTH:] = (yi * c - yr * s).astype(BF16)


def _radix4(u, batch):
    z = u.reshape(batch, 4, DFT_N, 2 * F_WIDTH)
    twc, tws = _twiddle_tables(4)
    twc, tws = jnp.asarray(twc), jnp.asarray(tws)
    nr = DFT_N // RADIX_ROWS
    blk = pl.BlockSpec((None, 4, RADIX_ROWS, 2 * F_WIDTH), lambda b, i: (b, 0, i, 0))
    tw = pl.BlockSpec((4, RADIX_ROWS, LANES), lambda b, i: (0, i, 0))
    y = pl.pallas_call(
        _radix4_body,
        grid=(batch, nr),
        in_specs=[blk, tw, tw],
        out_specs=blk,
        out_shape=jax.ShapeDtypeStruct((batch, 4, DFT_N, 2 * F_WIDTH), BF16),
        compiler_params=pltpu.CompilerParams(dimension_semantics=("arbitrary", "arbitrary"),
                                             vmem_limit_bytes=VMEM_LIMIT),
        name="radix4",
    )(z, twc, tws)
    return y.reshape(batch * 4, DFT_N, 2 * F_WIDTH)


def _seqdft_body(cs_ref, ab_ref, o_ref, *, scale):
    a = ab_ref[:, 0:F_WIDTH]
    b = ab_ref[:, F_WIDTH:]
    o_ref[...] = (_dot(cs_ref[:, 0:DFT_N], a) + _dot(cs_ref[:, DFT_N:], b)) * scale


def _seqdft(ab, batch, radix):
    cs = jnp.asarray(_SEQ_DFT, BF16)
    scale = float(1.0 / np.sqrt(radix * DFT_N))
    out = pl.pallas_call(
        functools.partial(_seqdft_body, scale=scale),
        grid=(batch * radix, DFT_N // DFT_ROWS),
        in_specs=[pl.BlockSpec((DFT_ROWS, 2 * DFT_N), lambda n, j: (j, 0)),
                  pl.BlockSpec((None, DFT_N, 2 * F_WIDTH), lambda n, j: (n, 0, 0))],
        out_specs=pl.BlockSpec((None, DFT_ROWS, F_WIDTH), lambda n, j: (n // radix, j, n % radix)),
        out_shape=jax.ShapeDtypeStruct((batch, DFT_N, radix * F_WIDTH), F32),
        compiler_params=pltpu.CompilerParams(dimension_semantics=("arbitrary", "arbitrary"),
                                             vmem_limit_bytes=VMEM_LIMIT),
        name="seqdft",
    )(cs, ab)
    return out.reshape(batch * DFT_N * radix, F_WIDTH)


def _first_index(hit, rows):
    return jnp.min(jnp.where(hit, rows.astype(F32), 1e6), axis=0, keepdims=True).astype(jnp.int32)


def _postmix_body(of_ref, ob_ref, r_ref, fo_ref, x_ref, wo_ref, gg_ref, g2_ref, wr_ref,
                  h_ref, xn_ref, eidx_ref, cw_ref, rank_ref, cnt_ref, carry_ref):
    @pl.when(pl.program_id(0) == 0)
    def _():
        carry_ref[...] = jnp.zeros_like(carry_ref)

    o = of_ref[...] + ob_ref[...]
    r = r_ref[...].astype(F32)
    parts = []
    for hd in range(GLA_HEADS):
        sl = slice(hd * GLA_DV, (hd + 1) * GLA_DV)
        oh = o[:, sl]
        oh = oh * lax.rsqrt(jnp.mean(oh * oh, axis=-1, keepdims=True) + EPS)
        rh = r[:, sl]
        parts.append((oh * gg_ref[...] * (rh * jax.nn.sigmoid(rh))).astype(BF16))
    on = jnp.concatenate(parts, axis=1)
    mixed = _dot(on, wo_ref[0:GLA_WIDTH, :]) + _dot(fo_ref[...].astype(BF16), wo_ref[GLA_WIDTH:, :])
    h = x_ref[...] + mixed
    h_ref[...] = h
    xn = h * lax.rsqrt(jnp.mean(h * h, axis=-1, keepdims=True) + EPS) * g2_ref[...]
    _store_token_tiles(xn_ref, xn)
    logits = _dot_nt(wr_ref[...], xn.astype(BF16))

    sub8 = lax.broadcasted_iota(jnp.int32, (8, TM_TOK), 0)
    lg = jnp.where(sub8 < N_GROUPS, logits[0:8], -jnp.inf)
    gmax = jnp.max(lg, axis=0, keepdims=True)
    g_w = 1.0 / jnp.sum(jnp.exp(lg - gmax), axis=0, keepdims=True)
    g_sel = _first_index(lg == gmax, sub8)
    sel = logits[8:16]
    for g in range(1, N_GROUPS):
        sel = jnp.where(g_sel == g, logits[8 + 8 * g:16 + 8 * g], sel)
    m1 = jnp.max(sel, axis=0, keepdims=True)
    i1 = _first_index(sel == m1, sub8)
    sel2 = jnp.where(sub8 == i1, -jnp.inf, sel)
    m2 = jnp.max(sel2, axis=0, keepdims=True)
    i2 = _first_index(sel2 == m2, sub8)
    e21 = jnp.exp(m2 - m1)
    w1 = 1.0 / (1.0 + e21)
    w2 = e21 / (1.0 + e21)
    e1 = g_sel * EXPERTS_PER_GROUP + i1
    e2 = g_sel * EXPERTS_PER_GROUP + i2
    eidx_ref[...] = jnp.concatenate([e1, e2], axis=0)
    cw_ref[...] = jnp.concatenate([g_w * w1, g_w * w2], axis=0)

    sub = lax.broadcasted_iota(jnp.int32, (N_EXPERTS, TM_TOK), 0)
    oh1 = sub == e1
    oh2 = sub == e2
    oh1b = jnp.where(oh1, 1.0, 0.0).astype(BF16)
    oh2b = jnp.where(oh2, 1.0, 0.0).astype(BF16)
    before = (lax.broadcasted_iota(jnp.int32, (TM_TOK, TM_TOK), 0)
              < lax.broadcasted_iota(jnp.int32, (TM_TOK, TM_TOK), 1))
    before = jnp.where(before, 1.0, 0.0).astype(BF16)
    ones = jnp.ones((TM_TOK, LANES), BF16)
    p1 = _dot(oh1b, before)
    p2 = _dot(oh2b, before)
    c1 = _dot(oh1b, ones)
    c2 = _dot(oh2b, ones)
    carry = carry_ref[...]
    rep = TM_TOK // LANES
    base1 = jnp.concatenate([carry] * rep, axis=1)
    base2 = jnp.concatenate([carry + c1] * rep, axis=1)
    rk1 = jnp.sum(jnp.where(oh1, p1 + base1, 0.0), axis=0, keepdims=True)
    rk2 = jnp.sum(jnp.where(oh2, p2 + base2, 0.0), axis=0, keepdims=True)
    rank_ref[...] = jnp.concatenate([rk1, rk2], axis=0).astype(jnp.int32)
    carry = carry + c1 + c2
    carry_ref[...] = carry
    cnt_ref[...] = carry


def _postmix(of, ob, r, fo, x2, wo, gg, g2, wr):
    t = x2.shape[0]
    nt = t // TM_TOK
    tok = lambda w: pl.BlockSpec((TM_TOK, w), lambda i: (i, 0))
    full = lambda a: pl.BlockSpec(a.shape, lambda i: (0,) * a.ndim)
    lane2 = pl.BlockSpec((2, TM_TOK), lambda i: (0, i))
    return pl.pallas_call(
        _postmix_body,
        grid=(nt,),
        in_specs=[tok(GLA_WIDTH), tok(GLA_WIDTH), tok(GLA_WIDTH), tok(F_WIDTH), tok(D_MODEL),
                  full(wo), full(gg), full(g2), full(wr)],
        out_specs=[tok(D_MODEL), pl.BlockSpec((TM_TOK * TOK_ROWS, LANES), lambda i: (i, 0)), lane2, lane2, lane2,
                   pl.BlockSpec((N_EXPERTS, LANES), lambda i: (0, 0))],
        out_shape=[jax.ShapeDtypeStruct((t, D_MODEL), F32), jax.ShapeDtypeStruct((t * TOK_ROWS, LANES), F32),
                   jax.ShapeDtypeStruct((2, t), jnp.int32), jax.ShapeDtypeStruct((2, t), F32),
                   jax.ShapeDtypeStruct((2, t), jnp.int32), jax.ShapeDtypeStruct((N_EXPERTS, LANES), F32)],
        scratch_shapes=[pltpu.VMEM((N_EXPERTS, LANES), F32)],
        compiler_params=pltpu.CompilerParams(dimension_semantics=("arbitrary",), vmem_limit_bytes=VMEM_LIMIT),
        name="postmix",
    )(of, ob, r, fo, x2, wo, gg, g2, wr)


def _store_token_tiles(ref, val):
    n = val.shape[0]
    for c in range(TOK_ROWS):
        ref[pl.ds(c, n, stride=TOK_ROWS), :] = val[:, c * LANES:(c + 1) * LANES]


def _load_token_tiles(ref, n):
    return jnp.concatenate([ref[pl.ds(c, n, stride=TOK_ROWS), :] for c in range(TOK_ROWS)], axis=1)


def _token_rows(tok):
    return pl.ds(pl.multiple_of(tok * TOK_ROWS, TOK_ROWS), TOK_ROWS)


def _row_copy(src_hbm, dst_hbm, src_tok, dst_tok, sem):
    return pltpu.make_async_copy(src_hbm.at[_token_rows(src_tok)], dst_hbm.at[_token_rows(dst_tok)], sem)


def _dispatch_body(fill_ref, pos_ref, xn_hbm, xs_hbm, zeros_ref, fill_sem, row_sem):
    i = pl.program_id(0)

    @pl.when(i == 0)
    def _():
        zeros_ref[...] = jnp.zeros_like(zeros_ref)
        tile_rows = TM_EXP * TOK_ROWS

        def fill(e):
            rows = pl.ds(pl.multiple_of(fill_ref[e] * tile_rows, tile_rows), tile_rows)
            return pltpu.make_async_copy(zeros_ref, xs_hbm.at[rows], fill_sem)

        for e in range(N_FILL):
            @pl.when(fill_ref[e] >= 0)
            def _():
                fill(e).start()
        for e in range(N_FILL):
            @pl.when(fill_ref[e] >= 0)
            def _():
                fill(e).wait()

    base = i * TM_DISP

    def issue(j, carry):
        for s in range(2):
            _row_copy(xn_hbm, xs_hbm, base + j, pos_ref[s, j], row_sem).start()
        return carry

    lax.fori_loop(0, TM_DISP, issue, 0, unroll=8)

    def drain(j, carry):
        for s in range(2):
            _row_copy(xn_hbm, xs_hbm, base + j, pos_ref[s, j], row_sem).wait()
        return carry

    lax.fori_loop(0, TM_DISP, drain, 0, unroll=8)


def _dispatch(xn, pos, fill_rows, rows_total):
    t = xn.shape[0] // TOK_ROWS
    grid_spec = pltpu.PrefetchScalarGridSpec(
        num_scalar_prefetch=1,
        grid=(t // TM_DISP,),
        in_specs=[pl.BlockSpec((2, TM_DISP), lambda i, fill: (0, i), memory_space=pltpu.SMEM),
                  pl.BlockSpec(memory_space=pl.ANY)],
        out_specs=pl.BlockSpec(memory_space=pl.ANY),
        scratch_shapes=[pltpu.VMEM((TM_EXP * TOK_ROWS, LANES), F32), pltpu.SemaphoreType.DMA(()),
                        pltpu.SemaphoreType.DMA(())],
    )
    return pl.pallas_call(
        _dispatch_body,
        grid_spec=grid_spec,
        out_shape=jax.ShapeDtypeStruct((rows_total * TOK_ROWS, LANES), F32),
        compiler_params=pltpu.CompilerParams(dimension_semantics=("arbitrary",), vmem_limit_bytes=VMEM_LIMIT),
        name="dispatch",
    )(fill_rows, pos, xn)


def _experts_body(te_ref, na_ref, xs_ref, wg_ref, wu_ref, wd_ref, ys_ref):
    @pl.when(pl.program_id(0) < na_ref[0])
    def _():
        x = _load_token_tiles(xs_ref, TM_EXP).astype(BF16)
        gate = _dot(x, wg_ref[...])
        up = _dot(x, wu_ref[...])
        hid = (gate * jax.nn.sigmoid(gate) * up).astype(BF16)
        _store_token_tiles(ys_ref, _dot(hid, wd_ref[...]))

    @pl.when(pl.program_id(0) >= na_ref[0])
    def _():
        ys_ref[...] = jnp.zeros_like(ys_ref)


def _experts(xs, tile_expert, n_active, wg, wu, wd):
    n_tiles = xs.shape[0] // (TM_EXP * TOK_ROWS)
    row = lambda i, te, na: (i, 0)
    wsel = lambda i, te, na: (te[jnp.minimum(i, na[0] - 1)], 0, 0)
    grid_spec = pltpu.PrefetchScalarGridSpec(
        num_scalar_prefetch=2,
        grid=(n_tiles,),
        in_specs=[pl.BlockSpec((TM_EXP * TOK_ROWS, LANES), row),
                  pl.BlockSpec((None, D_MODEL, D_EXPERT), wsel),
                  pl.BlockSpec((None, D_MODEL, D_EXPERT), wsel),
                  pl.BlockSpec((None, D_EXPERT, D_MODEL), wsel)],
        out_specs=pl.BlockSpec((TM_EXP * TOK_ROWS, LANES), row),
    )
    return pl.pallas_call(
        _experts_body,
        grid_spec=grid_spec,
        out_shape=jax.ShapeDtypeStruct(xs.shape, F32),
        compiler_params=pltpu.CompilerParams(dimension_semantics=("arbitrary",), vmem_limit_bytes=VMEM_LIMIT),
        name="experts",
    )(tile_expert, n_active, xs, wg, wu, wd)


def _combine_body(pos_ref, ys_hbm, cw_ref, h_ref, gf_ref, y_ref, buf_ref, sem):
    def row_gather(s, j):
        return pltpu.make_async_copy(ys_hbm.at[_token_rows(pos_ref[s, j])], buf_ref.at[s, _token_rows(j)], sem)

    def issue(j, carry):
        for s in range(2):
            row_gather(s, j).start()
        return carry

    lax.fori_loop(0, TM_COMB, issue, 0, unroll=8)

    def drain(j, carry):
        for s in range(2):
            row_gather(s, j).wait()
        return carry

    lax.fori_loop(0, TM_COMB, drain, 0, unroll=8)

    cw = cw_ref[...]
    y0 = _load_token_tiles(buf_ref.at[0], TM_COMB)
    y1 = _load_token_tiles(buf_ref.at[1], TM_COMB)
    h = h_ref[...] + cw[:, 0:1] * y0 + cw[:, 1:2] * y1
    y_ref[...] = h * lax.rsqrt(jnp.mean(h * h, axis=-1, keepdims=True) + EPS) * gf_ref[...]


def _combine(ys, pos, cw_t, h, gf):
    t = h.shape[0]
    tok = pl.BlockSpec((TM_COMB, D_MODEL), lambda i: (i, 0))
    return pl.pallas_call(
        _combine_body,
        grid=(t // TM_COMB,),
        in_specs=[pl.BlockSpec((2, TM_COMB), lambda i: (0, i), memory_space=pltpu.SMEM),
                  pl.BlockSpec(memory_space=pl.ANY),
                  pl.BlockSpec((TM_COMB, 2), lambda i: (i, 0)),
                  tok,
                  pl.BlockSpec((1, D_MODEL), lambda i: (0, 0))],
        out_specs=tok,
        out_shape=jax.ShapeDtypeStruct((t, D_MODEL), F32),
        scratch_shapes=[pltpu.VMEM((2, TM_COMB * TOK_ROWS, LANES), F32), pltpu.SemaphoreType.DMA(())],
        compiler_params=pltpu.CompilerParams(dimension_semantics=("arbitrary",), vmem_limit_bytes=VMEM_LIMIT),
        name="combine",
    )(pos, ys, cw_t, h, gf)


def _prepare_params(norm1_g, w_in, w_gk2_f, b_gk_f, w_gk2_b, b_gk_b, gla_norm_g, w_out,
                    norm2_g, w_group, w_expert, w_gate, w_up, w_down, norm_f_g):
    w = w_in[0]
    gate_lo = 2 * GLA_KEY_WIDTH + 2 * GLA_WIDTH
    gate_hi = gate_lo + 2 * GATE_RANK
    w_in_r = jnp.concatenate([w[:, :gate_lo], w[:, gate_hi:], w[:, gate_lo:gate_hi],
                              jnp.zeros((D_MODEL, GATE_COLS - 2 * GATE_RANK), F32)], axis=1).astype(BF16)
    wg = jnp.zeros((GATE_COLS, 2 * GLA_KEY_WIDTH), F32)
    wg = wg.at[0:GATE_RANK, 0:GLA_KEY_WIDTH].set(w_gk2_f[0])
    wg = wg.at[GATE_RANK:2 * GATE_RANK, GLA_KEY_WIDTH:].set(w_gk2_b[0]).astype(BF16)
    bg = jnp.concatenate([b_gk_f[0], b_gk_b[0]])[None, :]
    wr = jnp.zeros((ROUTER_ROWS, D_MODEL), F32)
    wr = wr.at[0:N_GROUPS].set(w_group[0].T)
    wr = wr.at[8:8 + N_EXPERTS].set(w_expert[0].T).astype(BF16)
    return dict(
        g1=norm1_g[0][None, :], w_in_r=w_in_r, wg=wg, bg=bg,
        gg=gla_norm_g[0][None, :], wo=w_out[0].astype(BF16), g2=norm2_g[0][None, :], wr=wr,
        w_gate=w_gate[0].astype(BF16), w_up=w_up[0].astype(BF16), w_down=w_down[0].astype(BF16),
        gf=norm_f_g[None, :])


def _encoder(x, p):
    batch, seq, _ = x.shape
    t = batch * seq
    assert seq % DFT_N == 0 and seq // DFT_N in (1, 4), "sequence DFT supports seq = 2048 or 8192"
    radix = seq // DFT_N
    x2 = x.reshape(t, D_MODEL)
    qf, kf, tf, qb, kb, tb, v, r, u, decf, decb = _inproj(x2, p["g1"], p["w_in_r"], p["wg"], p["bg"])
    of, ob = _gla(qf, kf, tf, qb, kb, tb, v, decf, decb, batch, seq)
    if radix == 1:
        ab = u.reshape(batch, DFT_N, 2 * F_WIDTH)
    else:
        ab = _radix4(u, batch)
    fo = _seqdft(ab, batch, radix)
    h, xn, eidx, cw, rank, cnt = _postmix(of, ob, r, fo, x2, p["wo"], p["gg"], p["g2"], p["wr"])

    counts = cnt[:, 0].astype(jnp.int32)
    tiles = (counts + TM_EXP - 1) // TM_EXP
    tile_end = jnp.cumsum(tiles)
    tile_start = tile_end - tiles
    pos = tile_start[eidx] * TM_EXP + rank
    n_tiles = 2 * t // TM_EXP + N_EXPERTS
    tile_expert = jnp.minimum(jnp.searchsorted(tile_end, jnp.arange(n_tiles, dtype=jnp.int32), side="right"),
                              N_EXPERTS - 1).astype(jnp.int32)
    n_active = tile_end[-1:].astype(jnp.int32)
    tail = n_active + jnp.arange(N_EXPERTS, dtype=jnp.int32)
    fill_tiles = jnp.concatenate([jnp.where(tiles > 0, tile_end - 1, -1),
                                  jnp.where(tail < n_tiles, tail, -1)]).astype(jnp.int32)

    xs = _dispatch(xn, pos, fill_tiles, n_tiles * TM_EXP)
    ys = _experts(xs, tile_expert, n_active, p["w_gate"], p["w_up"], p["w_down"])
    y = _combine(ys, pos, cw.T, h, p["gf"])
    return y.reshape(batch, seq, D_MODEL)


def kernel(x_prompt, x_sample, norm1_g, w_in, w_gk2_f, b_gk_f, w_gk2_b, b_gk_b, gla_norm_g, w_out, norm2_g,
           w_group, w_expert, w_gate, w_up, w_down, norm_f_g):
    p = _prepare_params(norm1_g, w_in, w_gk2_f, b_gk_f, w_gk2_b, b_gk_b, gla_norm_g, w_out,
                        norm2_g, w_group, w_expert, w_gate, w_up, w_down, norm_f_g)
    return (_encoder(x_prompt, p), _encoder(x_sample, p))
```

```python
import functools

import numpy as np
import jax
import jax.numpy as jnp
from jax import lax
from jax.experimental import pallas as pl
from jax.experimental.pallas import tpu as pltpu

D_MODEL = 1024
EPS = 1e-6
GLA_HEADS = 4
GLA_DV = 128
GLA_DK = 64
GLA_WIDTH = GLA_HEADS * GLA_DV
GLA_KEY_WIDTH = GLA_HEADS * GLA_DK
GATE_RANK = 16
GATE_NORMALIZER = 16.0
CHUNK = 64
F_GROUPS = 4
F_GROUP_DIM = 128
F_WIDTH = F_GROUPS * F_GROUP_DIM
N_GROUPS = 4
EXPERTS_PER_GROUP = 8
N_EXPERTS = N_GROUPS * EXPERTS_PER_GROUP
D_EXPERT = 256

LANES = 128
V7X_VMEM_BYTES = 64 * 1024 * 1024
VMEM_LIMIT = 56 * 1024 * 1024

TM_TOK = 512
SUB = 128
DFT_N = 2048
DFT_ROWS = 512
TM_EXP = 512
TM_DISP = 512
TM_COMB = 256
ROUTER_ROWS = 48
GATE_COLS = 128
IN_COLS_PAD = 2048 + GATE_COLS
TOK_ROWS = D_MODEL // LANES
N_FILL = 2 * N_EXPERTS

BF16 = jnp.bfloat16
F32 = jnp.float32


def _dot(a, b):
    return jnp.dot(a, b, preferred_element_type=F32)


def _dot_nt(a, b):
    return lax.dot_general(a, b, (((1,), (1,)), ((), ())), preferred_element_type=F32)


def _dot_tn(a, b):
    return lax.dot_general(a, b, (((0,), (0,)), ((), ())), preferred_element_type=F32)


def _tri_tables():
    r = np.arange(SUB)
    same = (r[:, None] // CHUNK) == (r[None, :] // CHUNK)
    l_incl = same & (r[None, :] <= r[:, None])
    u_strict = same & (r[None, :] > r[:, None])
    u_incl = same & (r[None, :] >= r[:, None])
    l_strict = same & (r[None, :] < r[:, None])
    fwd = np.concatenate([l_incl, u_strict], 0).astype(np.float32)
    bwd = np.concatenate([u_incl, l_strict], 0).astype(np.float32)
    return fwd, bwd


def _chan_dft_table():
    c = np.arange(F_GROUP_DIM)
    ang = 2.0 * np.pi * ((c[:, None] * c[None, :]) % F_GROUP_DIM) / F_GROUP_DIM
    s = 1.0 / np.sqrt(F_GROUP_DIM)
    return np.concatenate([np.cos(ang) * s, -np.sin(ang) * s], 1).astype(np.float32)


def _seq_dft_table():
    k = np.arange(DFT_N, dtype=np.int64)
    ang = 2.0 * np.pi * ((k[:, None] * k[None, :]) % DFT_N) / DFT_N
    return np.concatenate([np.cos(ang), np.sin(ang)], 1).astype(np.float32)


def _twiddle_tables(radix):
    k1 = np.arange(radix, dtype=np.int64)[:, None]
    s2 = np.arange(DFT_N, dtype=np.int64)[None, :]
    ang = 2.0 * np.pi * ((k1 * s2) % (radix * DFT_N)) / (radix * DFT_N)
    c = np.repeat(np.cos(ang)[:, :, None], LANES, 2).astype(np.float32)
    s = np.repeat(np.sin(ang)[:, :, None], LANES, 2).astype(np.float32)
    return c, s


_TRI_F, _TRI_B = _tri_tables()
_CHAN_DFT = _chan_dft_table()
_SEQ_DFT = _seq_dft_table()


def _inproj_body(x_ref, g1_ref, w_ref, wg_ref, bg_ref, trif_ref, trib_ref, cs_ref,
                 qf_ref, kf_ref, tf_ref, qb_ref, kb_ref, tb_ref, v_ref, r_ref, u_ref,
                 decf_ref, decb_ref, tot_ref):
    x = x_ref[...]
    ms = jnp.mean(x * x, axis=-1, keepdims=True)
    xn = (x * lax.rsqrt(ms + EPS) * g1_ref[...]).astype(BF16)
    qk = _dot(xn, w_ref[:, 0:2 * GLA_KEY_WIDTH])
    v_ref[...] = _dot(xn, w_ref[:, 512:1024]).astype(BF16)
    r_ref[...] = _dot(xn, w_ref[:, 1024:1536]).astype(BF16)
    fx = _dot(xn, w_ref[:, 1536:2048]).astype(BF16)
    gates = _dot(xn, w_ref[:, 2048:IN_COLS_PAD]).astype(BF16)
    z = _dot(gates, wg_ref[...]) + bg_ref[...]
    la = (jnp.minimum(z, 0.0) - jnp.log1p(jnp.exp(-jnp.abs(z)))) * (1.0 / GATE_NORMALIZER)
    la_hi = la.astype(BF16)
    la_lo = (la - la_hi.astype(F32)).astype(BF16)
    trif = trif_ref[...]
    trib = trib_ref[...]
    scale = GLA_DK ** -0.5
    for s in range(TM_TOK // SUB):
        rows = slice(s * SUB, (s + 1) * SUB)
        q = qk[rows, 0:GLA_KEY_WIDTH]
        k = qk[rows, GLA_KEY_WIDTH:2 * GLA_KEY_WIDTH]
        rf = _dot(trif, la_hi[rows, 0:GLA_KEY_WIDTH]) + _dot(trif, la_lo[rows, 0:GLA_KEY_WIDTH])
        b, tl = rf[0:SUB], rf[SUB:2 * SUB]
        qf_ref[rows, :] = (q * scale * jnp.exp(b)).astype(BF16)
        kf_ref[rows, :] = (k * jnp.exp(-b)).astype(BF16)
        tf_ref[rows, :] = (k * jnp.exp(tl)).astype(BF16)
        totf = b + tl
        tot_ref[0, rows, :] = totf[:, 0:LANES]
        tot_ref[1, rows, :] = totf[:, LANES:]
        rb = _dot(trib, la_hi[rows, GLA_KEY_WIDTH:]) + _dot(trib, la_lo[rows, GLA_KEY_WIDTH:])
        c, tlb = rb[0:SUB], rb[SUB:2 * SUB]
        qb_ref[rows, :] = (q * scale * jnp.exp(c)).astype(BF16)
        kb_ref[rows, :] = (k * jnp.exp(-c)).astype(BF16)
        tb_ref[rows, :] = (k * jnp.exp(tlb)).astype(BF16)
        totb = c + tlb
        tot_ref[2, rows, :] = totb[:, 0:LANES]
        tot_ref[3, rows, :] = totb[:, LANES:]
    chunk_rows = pl.ds(0, TM_TOK // CHUNK, stride=CHUNK)
    decf_ref[:, 0:LANES] = jnp.exp(tot_ref[0, chunk_rows, :])
    decf_ref[:, LANES:] = jnp.exp(tot_ref[1, chunk_rows, :])
    decb_ref[:, 0:LANES] = jnp.exp(tot_ref[2, chunk_rows, :])
    decb_ref[:, LANES:] = jnp.exp(tot_ref[3, chunk_rows, :])
    cs = cs_ref[...]
    for g in range(F_GROUPS):
        res = _dot(fx[:, g * LANES:(g + 1) * LANES], cs)
        u_ref[:, g * LANES:(g + 1) * LANES] = res[:, 0:LANES].astype(BF16)
        u_ref[:, F_WIDTH + g * LANES:F_WIDTH + (g + 1) * LANES] = res[:, LANES:].astype(BF16)


def _inproj(x2, g1, w_in_r, wg, bg, trif, trib, cs):
    t = x2.shape[0]
    nt = t // TM_TOK
    tok = lambda w: pl.BlockSpec((TM_TOK, w), lambda i: (i, 0))
    full = lambda a: pl.BlockSpec(a.shape, lambda i: (0,) * a.ndim)
    kw = jax.ShapeDtypeStruct((t, GLA_KEY_WIDTH), BF16)
    dec = jax.ShapeDtypeStruct((t // CHUNK, GLA_KEY_WIDTH), F32)
    dec_spec = pl.BlockSpec((TM_TOK // CHUNK, GLA_KEY_WIDTH), lambda i: (i, 0))
    return pl.pallas_call(
        _inproj_body,
        grid=(nt,),
        in_specs=[tok(D_MODEL), full(g1), full(w_in_r), full(wg), full(bg), full(trif), full(trib), full(cs)],
        out_specs=[tok(GLA_KEY_WIDTH)] * 6 + [tok(GLA_WIDTH), tok(GLA_WIDTH), tok(2 * F_WIDTH), dec_spec, dec_spec],
        out_shape=[kw] * 6 + [jax.ShapeDtypeStruct((t, GLA_WIDTH), BF16), jax.ShapeDtypeStruct((t, GLA_WIDTH), BF16),
                              jax.ShapeDtypeStruct((t, 2 * F_WIDTH), BF16), dec, dec],
        scratch_shapes=[pltpu.VMEM((2 * GLA_KEY_WIDTH // LANES, TM_TOK, LANES), F32)],
        compiler_params=pltpu.CompilerParams(dimension_semantics=("arbitrary",), vmem_limit_bytes=VMEM_LIMIT),
        name="inproj",
    )(x2, g1, w_in_r, wg, bg, trif, trib, cs)


def _gla_chunk(q_ref, k_ref, t_ref, v_ref, d_ref, o_ref, s_ref, c, causal, m_lo, mv_lo, bd):
    rows = slice(c * CHUNK, (c + 1) * CHUNK)
    dec = d_ref[c:c + 1, :]
    for p in range(GLA_HEADS // 2):
        kl = slice(p * LANES, (p + 1) * LANES)
        vl = slice(p * 2 * GLA_DV, (p + 1) * 2 * GLA_DV)
        qd = q_ref[rows, kl]
        kd = k_ref[rows, kl]
        kt = t_ref[rows, kl]
        vv = v_ref[rows, vl]
        zk = jnp.zeros_like(kd)
        zv = jnp.zeros_like(vv)
        kbd = jnp.concatenate([jnp.where(m_lo, kd, zk), jnp.where(m_lo, zk, kd)], axis=0)
        att = _dot_nt(qd, kbd)
        att = jnp.where(causal, att, 0.0).astype(BF16)
        vbd = jnp.concatenate([jnp.where(mv_lo, vv, zv), jnp.where(mv_lo, zv, vv)], axis=0)
        st = s_ref[p]
        o_ref[rows, vl] = (_dot(att, vbd) + _dot_nt(qd, st.astype(BF16))).astype(o_ref.dtype)
        kv = _dot_tn(vv, kt)
        s_ref[p] = st * dec[:, kl] + jnp.where(bd, kv, 0.0)


def _gla_body(qf_ref, kf_ref, tf_ref, vf_ref, df_ref, qb_ref, kb_ref, tb_ref, vb_ref, db_ref,
              of_ref, ob_ref, sf_ref, sb_ref):
    @pl.when(pl.program_id(1) == 0)
    def _():
        sf_ref[...] = jnp.zeros_like(sf_ref)
        sb_ref[...] = jnp.zeros_like(sb_ref)

    lane = lax.broadcasted_iota(jnp.int32, (CHUNK, LANES), 1)
    row = lax.broadcasted_iota(jnp.int32, (CHUNK, LANES), 0)
    m_lo = lane < GLA_DK
    col = lane & (CHUNK - 1)
    causal_f = row >= col
    causal_b = row <= col
    mv_lo = lax.broadcasted_iota(jnp.int32, (CHUNK, 2 * GLA_DV), 1) < GLA_DV
    bd = ((lax.broadcasted_iota(jnp.int32, (2 * GLA_DV, LANES), 0) < GLA_DV)
          == (lax.broadcasted_iota(jnp.int32, (2 * GLA_DV, LANES), 1) < GLA_DK))
    n = TM_TOK // CHUNK
    for j in range(n):
        _gla_chunk(qf_ref, kf_ref, tf_ref, vf_ref, df_ref, of_ref, sf_ref, j, causal_f, m_lo, mv_lo, bd)
        _gla_chunk(qb_ref, kb_ref, tb_ref, vb_ref, db_ref, ob_ref, sb_ref, n - 1 - j, causal_b, m_lo, mv_lo, bd)


def _gla(qf, kf, tf, qb, kb, tb, v, decf, decb, batch, seq):
    t = batch * seq
    nt = seq // TM_TOK
    fwd = lambda b, i: (b * nt + i, 0)
    bwd = lambda b, i: (b * nt + nt - 1 - i, 0)
    ks = lambda m: pl.BlockSpec((TM_TOK, GLA_KEY_WIDTH), m)
    vs = lambda m: pl.BlockSpec((TM_TOK, GLA_WIDTH), m)
    ds = lambda m: pl.BlockSpec((TM_TOK // CHUNK, GLA_KEY_WIDTH), m)
    o = jax.ShapeDtypeStruct((t, GLA_WIDTH), BF16)
    state = pltpu.VMEM((GLA_HEADS // 2, 2 * GLA_DV, LANES), F32)
    return pl.pallas_call(
        _gla_body,
        grid=(batch, nt),
        in_specs=[ks(fwd), ks(fwd), ks(fwd), vs(fwd), ds(fwd), ks(bwd), ks(bwd), ks(bwd), vs(bwd), ds(bwd)],
        out_specs=[vs(fwd), vs(bwd)],
        out_shape=[o, o],
        scratch_shapes=[state, state],
        compiler_params=pltpu.CompilerParams(dimension_semantics=("arbitrary", "arbitrary"),
                                             vmem_limit_bytes=VMEM_LIMIT),
        name="gla",
    )(qf, kf, tf, v, decf, qb, kb, tb, v, decb)


RADIX_ROWS = 256


def _radix4_body(z_ref, twc_ref, tws_ref, y_ref):
    z = [z_ref[s].astype(F32) for s in range(4)]
    re = [a[:, 0:F_WIDTH] for a in z]
    im = [a[:, F_WIDTH:] for a in z]
    ar, ai = re[0] + re[2], im[0] + im[2]
    br, bi = re[0] - re[2], im[0] - im[2]
    cr, ci = re[1] + re[3], im[1] + im[3]
    dr, di = re[1] - re[3], im[1] - im[3]
    y = [(ar + cr, ai + ci), (br + di, bi - dr), (ar - cr, ai - ci), (br - di, bi + dr)]
    y_ref[0, :, 0:F_WIDTH] = y[0][0].astype(BF16)
    y_ref[0, :, F_WIDTH:] = y[0][1].astype(BF16)
    for k1 in range(1, 4):
        c = jnp.concatenate([twc_ref[k1]] * (F_WIDTH // LANES), axis=1)
        s = jnp.concatenate([tws_ref[k1]] * (F_WIDTH // LANES), axis=1)
        yr, yi = y[k1]
        y_ref[k1, :, 0:F_WIDTH] = (yr * c + yi * s).astype(BF16)
        y_ref[k1, :, F_WIDTH:] = (yi * c - yr * s).astype(BF16)


def _radix4(u, batch):
    z = u.reshape(batch, 4, DFT_N, 2 * F_WIDTH)
    twc, tws = _twiddle_tables(4)
    twc, tws = jnp.asarray(twc), jnp.asarray(tws)
    nr = DFT_N // RADIX_ROWS
    blk = pl.BlockSpec((None, 4, RADIX_ROWS, 2 * F_WIDTH), lambda b, i: (b, 0, i, 0))
    tw = pl.BlockSpec((4, RADIX_ROWS, LANES), lambda b, i: (0, i, 0))
    y = pl.pallas_call(
        _radix4_body,
        grid=(batch, nr),
        in_specs=[blk, tw, tw],
        out_specs=blk,
        out_shape=jax.ShapeDtypeStruct((batch, 4, DFT_N, 2 * F_WIDTH), BF16),
        compiler_params=pltpu.CompilerParams(dimension_semantics=("arbitrary", "arbitrary"),
                                             vmem_limit_bytes=VMEM_LIMIT),
        name="radix4",
    )(z, twc, tws)
    return y


def _seqdft_body(cs_ref, ab_ref, o_ref, *, radix, scale):
    for k1 in range(radix):
        a = ab_ref[k1, :, 0:F_WIDTH]
        b = ab_ref[k1, :, F_WIDTH:]
        res = (_dot(cs_ref[:, 0:DFT_N], a) + _dot(cs_ref[:, DFT_N:], b)) * scale
        for c in range(F_WIDTH // LANES):
            o_ref[c, pl.ds(k1, DFT_ROWS, stride=radix), :] = res[:, c * LANES:(c + 1) * LANES]


def _seqdft(ab, cs, batch, radix):
    scale = float(1.0 / np.sqrt(radix * DFT_N))
    nj = DFT_N // DFT_ROWS
    return pl.pallas_call(
        functools.partial(_seqdft_body, radix=radix, scale=scale),
        grid=(batch, nj),
        in_specs=[pl.BlockSpec((DFT_ROWS, 2 * DFT_N), lambda b, j: (j, 0)),
                  pl.BlockSpec((None, radix, DFT_N, 2 * F_WIDTH), lambda b, j: (b, 0, 0, 0))],
        out_specs=pl.BlockSpec((F_WIDTH // LANES, radix * DFT_ROWS, LANES), lambda b, j: (0, b * nj + j, 0)),
        out_shape=jax.ShapeDtypeStruct((F_WIDTH // LANES, batch * radix * DFT_N, LANES), F32),
        compiler_params=pltpu.CompilerParams(dimension_semantics=("arbitrary", "arbitrary"),
                                             vmem_limit_bytes=VMEM_LIMIT),
        name="seqdft",
    )(cs, ab)


def _first_index(hit, rows):
    return jnp.min(jnp.where(hit, rows.astype(F32), 1e6), axis=0, keepdims=True).astype(jnp.int32)


def _postmix_body(of_ref, ob_ref, r_ref, fo_ref, x_ref, wo_ref, gg_ref, g2_ref, wr_ref,
                  h_ref, xn_ref, eidx_ref, cw_ref, rank_ref, cnt_ref, carry_ref):
    @pl.when(pl.program_id(0) == 0)
    def _():
        carry_ref[...] = jnp.zeros_like(carry_ref)

    o = of_ref[...].astype(F32) + ob_ref[...].astype(F32)
    r = r_ref[...].astype(F32)
    parts = []
    for hd in range(GLA_HEADS):
        sl = slice(hd * GLA_DV, (hd + 1) * GLA_DV)
        oh = o[:, sl]
        oh = oh * lax.rsqrt(jnp.mean(oh * oh, axis=-1, keepdims=True) + EPS)
        rh = r[:, sl]
        parts.append((oh * gg_ref[...] * (rh * jax.nn.sigmoid(rh))).astype(BF16))
    on = jnp.concatenate(parts, axis=1)
    fo = jnp.concatenate([fo_ref[c] for c in range(F_WIDTH // LANES)], axis=1)
    mixed = _dot(on, wo_ref[0:GLA_WIDTH, :]) + _dot(fo.astype(BF16), wo_ref[GLA_WIDTH:, :])
    h = x_ref[...] + mixed
    h_ref[...] = h
    xn = h * lax.rsqrt(jnp.mean(h * h, axis=-1, keepdims=True) + EPS) * g2_ref[...]
    _store_token_tiles(xn_ref, xn)
    logits = _dot_nt(wr_ref[...], xn.astype(BF16))

    sub8 = lax.broadcasted_iota(jnp.int32, (8, TM_TOK), 0)
    lg = jnp.where(sub8 < N_GROUPS, logits[0:8], -jnp.inf)
    gmax = jnp.max(lg, axis=0, keepdims=True)
    g_w = 1.0 / jnp.sum(jnp.exp(lg - gmax), axis=0, keepdims=True)
    g_sel = _first_index(lg == gmax, sub8)
    sel = logits[8:16]
    for g in range(1, N_GROUPS):
        sel = jnp.where(g_sel == g, logits[8 + 8 * g:16 + 8 * g], sel)
    m1 = jnp.max(sel, axis=0, keepdims=True)
    i1 = _first_index(sel == m1, sub8)
    sel2 = jnp.where(sub8 == i1, -jnp.inf, sel)
    m2 = jnp.max(sel2, axis=0, keepdims=True)
    i2 = _first_index(sel2 == m2, sub8)
    e21 = jnp.exp(m2 - m1)
    w1 = 1.0 / (1.0 + e21)
    w2 = e21 / (1.0 + e21)
    e1 = g_sel * EXPERTS_PER_GROUP + i1
    e2 = g_sel * EXPERTS_PER_GROUP + i2
    eidx_ref[...] = jnp.concatenate([e1, e2], axis=0)
    cw_ref[...] = jnp.concatenate([g_w * w1, g_w * w2], axis=0)

    sub = lax.broadcasted_iota(jnp.int32, (N_EXPERTS, TM_TOK), 0)
    oh1 = sub == e1
    oh2 = sub == e2
    oh1b = jnp.where(oh1, 1.0, 0.0).astype(BF16)
    oh2b = jnp.where(oh2, 1.0, 0.0).astype(BF16)
    before = (lax.broadcasted_iota(jnp.int32, (TM_TOK, TM_TOK), 0)
              < lax.broadcasted_iota(jnp.int32, (TM_TOK, TM_TOK), 1))
    before = jnp.where(before, 1.0, 0.0).astype(BF16)
    ones = jnp.ones((TM_TOK, LANES), BF16)
    p1 = _dot(oh1b, before)
    p2 = _dot(oh2b, before)
    c1 = _dot(oh1b, ones)
    c2 = _dot(oh2b, ones)
    carry = carry_ref[...]
    rep = TM_TOK // LANES
    base1 = jnp.concatenate([carry] * rep, axis=1)
    base2 = jnp.concatenate([carry + c1] * rep, axis=1)
    rk1 = jnp.sum(jnp.where(oh1, p1 + base1, 0.0), axis=0, keepdims=True)
    rk2 = jnp.sum(jnp.where(oh2, p2 + base2, 0.0), axis=0, keepdims=True)
    rank_ref[...] = jnp.concatenate([rk1, rk2], axis=0).astype(jnp.int32)
    carry = carry + c1 + c2
    carry_ref[...] = carry
    cnt_ref[...] = carry


def _postmix(of, ob, r, fo, x2, wo, gg, g2, wr):
    t = x2.shape[0]
    nt = t // TM_TOK
    tok = lambda w: pl.BlockSpec((TM_TOK, w), lambda i: (i, 0))
    full = lambda a: pl.BlockSpec(a.shape, lambda i: (0,) * a.ndim)
    lane2 = pl.BlockSpec((2, TM_TOK), lambda i: (0, i))
    return pl.pallas_call(
        _postmix_body,
        grid=(nt,),
        in_specs=[tok(GLA_WIDTH), tok(GLA_WIDTH), tok(GLA_WIDTH),
                  pl.BlockSpec((F_WIDTH // LANES, TM_TOK, LANES), lambda i: (0, i, 0)), tok(D_MODEL),
                  full(wo), full(gg), full(g2), full(wr)],
        out_specs=[tok(D_MODEL), pl.BlockSpec((TM_TOK * TOK_ROWS, LANES), lambda i: (i, 0)), lane2, lane2, lane2,
                   pl.BlockSpec((N_EXPERTS, LANES), lambda i: (0, 0))],
        out_shape=[jax.ShapeDtypeStruct((t, D_MODEL), F32), jax.ShapeDtypeStruct((t * TOK_ROWS, LANES), F32),
                   jax.ShapeDtypeStruct((2, t), jnp.int32), jax.ShapeDtypeStruct((2, t), F32),
                   jax.ShapeDtypeStruct((2, t), jnp.int32), jax.ShapeDtypeStruct((N_EXPERTS, LANES), F32)],
        scratch_shapes=[pltpu.VMEM((N_EXPERTS, LANES), F32)],
        compiler_params=pltpu.CompilerParams(dimension_semantics=("arbitrary",), vmem_limit_bytes=VMEM_LIMIT),
        name="postmix",
    )(of, ob, r, fo, x2, wo, gg, g2, wr)


def _store_token_tiles(ref, val):
    n = val.shape[0]
    for c in range(TOK_ROWS):
        ref[pl.ds(c, n, stride=TOK_ROWS), :] = val[:, c * LANES:(c + 1) * LANES]


def _load_token_tiles(ref, n):
    return jnp.concatenate([ref[pl.ds(c, n, stride=TOK_ROWS), :] for c in range(TOK_ROWS)], axis=1)


def _token_rows(tok):
    return pl.ds(pl.multiple_of(tok * TOK_ROWS, TOK_ROWS), TOK_ROWS)


def _row_copy(src_hbm, dst_hbm, src_tok, dst_tok, sem):
    return pltpu.make_async_copy(src_hbm.at[_token_rows(src_tok)], dst_hbm.at[_token_rows(dst_tok)], sem)


def _dispatch_body(fill_ref, pos_ref, xn_ref, xs_hbm, zeros_ref, fill_sem, row_sem):
    i = pl.program_id(0)

    @pl.when(i == 0)
    def _():
        zeros_ref[...] = jnp.zeros_like(zeros_ref)
        tile_rows = TM_EXP * TOK_ROWS

        def fill(e):
            rows = pl.ds(pl.multiple_of(fill_ref[e] * tile_rows, tile_rows), tile_rows)
            return pltpu.make_async_copy(zeros_ref, xs_hbm.at[rows], fill_sem)

        for e in range(N_FILL):
            @pl.when(fill_ref[e] >= 0)
            def _():
                fill(e).start()
        for e in range(N_FILL):
            @pl.when(fill_ref[e] >= 0)
            def _():
                fill(e).wait()

    def issue(j, carry):
        for s in range(2):
            _row_copy(xn_ref, xs_hbm, j, pos_ref[s, j], row_sem).start(priority=s)
        return carry

    lax.fori_loop(0, TM_DISP, issue, 0, unroll=8)

    def drain(j, carry):
        for s in range(2):
            _row_copy(xn_ref, xs_hbm, j, pos_ref[s, j], row_sem).wait()
        return carry

    lax.fori_loop(0, TM_DISP, drain, 0, unroll=8)


def _dispatch(xn, pos, fill_rows, rows_total):
    t = xn.shape[0] // TOK_ROWS
    grid_spec = pltpu.PrefetchScalarGridSpec(
        num_scalar_prefetch=1,
        grid=(t // TM_DISP,),
        in_specs=[pl.BlockSpec((2, TM_DISP), lambda i, fill: (0, i), memory_space=pltpu.SMEM),
                  pl.BlockSpec((TM_DISP * TOK_ROWS, LANES), lambda i, fill: (i, 0))],
        out_specs=pl.BlockSpec(memory_space=pl.ANY),
        scratch_shapes=[pltpu.VMEM((TM_EXP * TOK_ROWS, LANES), F32), pltpu.SemaphoreType.DMA(()),
                        pltpu.SemaphoreType.DMA(())],
    )
    return pl.pallas_call(
        _dispatch_body,
        grid_spec=grid_spec,
        out_shape=jax.ShapeDtypeStruct((rows_total * TOK_ROWS, LANES), F32),
        compiler_params=pltpu.CompilerParams(dimension_semantics=("arbitrary",), vmem_limit_bytes=VMEM_LIMIT),
        name="dispatch",
    )(fill_rows, pos, xn)


def _experts_body(te_ref, na_ref, xs_ref, wg_ref, wu_ref, wd_ref, ys_ref):
    @pl.when(pl.program_id(0) < na_ref[0])
    def _():
        x = _load_token_tiles(xs_ref, TM_EXP).astype(BF16)
        gate = _dot(x, wg_ref[...])
        up = _dot(x, wu_ref[...])
        hid = (gate * jax.nn.sigmoid(gate) * up).astype(BF16)
        _store_token_tiles(ys_ref, _dot(hid, wd_ref[...]))

    @pl.when(pl.program_id(0) >= na_ref[0])
    def _():
        ys_ref[...] = jnp.zeros_like(ys_ref)


def _experts(xs, tile_expert, n_active, wg, wu, wd):
    n_tiles = xs.shape[0] // (TM_EXP * TOK_ROWS)
    row = lambda i, te, na: (i, 0)
    wsel = lambda i, te, na: (te[jnp.minimum(i, na[0] - 1)], 0, 0)
    grid_spec = pltpu.PrefetchScalarGridSpec(
        num_scalar_prefetch=2,
        grid=(n_tiles,),
        in_specs=[pl.BlockSpec((TM_EXP * TOK_ROWS, LANES), row),
                  pl.BlockSpec((None, D_MODEL, D_EXPERT), wsel),
                  pl.BlockSpec((None, D_MODEL, D_EXPERT), wsel),
                  pl.BlockSpec((None, D_EXPERT, D_MODEL), wsel)],
        out_specs=pl.BlockSpec((TM_EXP * TOK_ROWS, LANES), row),
    )
    return pl.pallas_call(
        _experts_body,
        grid_spec=grid_spec,
        out_shape=jax.ShapeDtypeStruct(xs.shape, F32),
        compiler_params=pltpu.CompilerParams(dimension_semantics=("arbitrary",), vmem_limit_bytes=VMEM_LIMIT),
        name="experts",
    )(tile_expert, n_active, xs, wg, wu, wd)


def _combine_body(pos_ref, nxt_ref, ys_hbm, cw_ref, h_ref, gf_ref, y_ref, buf_ref, sem):
    i = pl.program_id(0)
    slot = lax.rem(i, 2)

    def row_gather(p_ref, sl, s, j):
        return pltpu.make_async_copy(ys_hbm.at[_token_rows(p_ref[s, j])], buf_ref.at[sl, s, _token_rows(j)],
                                     sem.at[sl])

    def issue_tile(p_ref, sl):
        def issue(j, carry):
            for s in range(2):
                row_gather(p_ref, sl, s, j).start(priority=s)
            return carry

        lax.fori_loop(0, TM_COMB, issue, 0, unroll=8)

    @pl.when(i == 0)
    def _():
        issue_tile(pos_ref, slot)

    @pl.when(i + 1 < pl.num_programs(0))
    def _():
        issue_tile(nxt_ref, 1 - slot)

    def drain(j, carry):
        for s in range(2):
            row_gather(pos_ref, slot, s, j).wait()
        return carry

    lax.fori_loop(0, TM_COMB, drain, 0, unroll=8)

    cw = cw_ref[...]
    y0 = _load_token_tiles(buf_ref.at[slot, 0], TM_COMB)
    y1 = _load_token_tiles(buf_ref.at[slot, 1], TM_COMB)
    h = h_ref[...] + cw[:, 0:1] * y0 + cw[:, 1:2] * y1
    y_ref[...] = h * lax.rsqrt(jnp.mean(h * h, axis=-1, keepdims=True) + EPS) * gf_ref[...]


def _combine(ys, pos, cw_t, h, gf):
    t = h.shape[0]
    n = t // TM_COMB
    tok = pl.BlockSpec((TM_COMB, D_MODEL), lambda i: (i, 0))
    return pl.pallas_call(
        _combine_body,
        grid=(n,),
        in_specs=[pl.BlockSpec((2, TM_COMB), lambda i: (0, i), memory_space=pltpu.SMEM),
                  pl.BlockSpec((2, TM_COMB), lambda i: (0, jnp.minimum(i + 1, n - 1)), memory_space=pltpu.SMEM),
                  pl.BlockSpec(memory_space=pl.ANY),
                  pl.BlockSpec((TM_COMB, 2), lambda i: (i, 0)),
                  tok,
                  pl.BlockSpec((1, D_MODEL), lambda i: (0, 0))],
        out_specs=tok,
        out_shape=jax.ShapeDtypeStruct((t, D_MODEL), F32),
        scratch_shapes=[pltpu.VMEM((2, 2, TM_COMB * TOK_ROWS, LANES), F32), pltpu.SemaphoreType.DMA((2,))],
        compiler_params=pltpu.CompilerParams(dimension_semantics=("arbitrary",), vmem_limit_bytes=VMEM_LIMIT),
        name="combine",
    )(pos, pos, ys, cw_t, h, gf)


def _prepare_params(norm1_g, w_in, w_gk2_f, b_gk_f, w_gk2_b, b_gk_b, gla_norm_g, w_out,
                    norm2_g, w_group, w_expert, w_gate, w_up, w_down, norm_f_g):
    w = w_in[0]
    gate_lo = 2 * GLA_KEY_WIDTH + 2 * GLA_WIDTH
    gate_hi = gate_lo + 2 * GATE_RANK
    w_in_r = jnp.concatenate([w[:, :gate_lo], w[:, gate_hi:], w[:, gate_lo:gate_hi],
                              jnp.zeros((D_MODEL, GATE_COLS - 2 * GATE_RANK), F32)], axis=1).astype(BF16)
    zk = jnp.zeros((GATE_RANK, GLA_KEY_WIDTH), F32)
    wg = jnp.concatenate([jnp.concatenate([w_gk2_f[0], zk], axis=1), jnp.concatenate([zk, w_gk2_b[0]], axis=1),
                          jnp.zeros((GATE_COLS - 2 * GATE_RANK, 2 * GLA_KEY_WIDTH), F32)], axis=0).astype(BF16)
    bg = jnp.concatenate([b_gk_f[0], b_gk_b[0]])[None, :]
    wr = jnp.concatenate([w_group[0].T, jnp.zeros((8 - N_GROUPS, D_MODEL), F32), w_expert[0].T,
                          jnp.zeros((ROUTER_ROWS - 8 - N_EXPERTS, D_MODEL), F32)], axis=0).astype(BF16)
    return dict(
        g1=norm1_g[0][None, :], w_in_r=w_in_r, wg=wg, bg=bg,
        gg=gla_norm_g[0][None, :], wo=w_out[0].astype(BF16), g2=norm2_g[0][None, :], wr=wr,
        w_gate=w_gate[0].astype(BF16), w_up=w_up[0].astype(BF16), w_down=w_down[0].astype(BF16),
        gf=norm_f_g[None, :],
        trif=jnp.asarray(_TRI_F).astype(BF16), trib=jnp.asarray(_TRI_B).astype(BF16),
        chan_dft=jnp.asarray(_CHAN_DFT).astype(BF16), seq_dft=jnp.asarray(_SEQ_DFT).astype(BF16))


def _encoder(x, p):
    batch, seq, _ = x.shape
    t = batch * seq
    assert seq % DFT_N == 0 and seq // DFT_N in (1, 4), "sequence DFT supports seq = 2048 or 8192"
    radix = seq // DFT_N
    x2 = x.reshape(t, D_MODEL)
    qf, kf, tf, qb, kb, tb, v, r, u, decf, decb = _inproj(x2, p["g1"], p["w_in_r"], p["wg"], p["bg"],
                                                          p["trif"], p["trib"], p["chan_dft"])
    of, ob = _gla(qf, kf, tf, qb, kb, tb, v, decf, decb, batch, seq)
    if radix == 1:
        ab = u.reshape(batch, 1, DFT_N, 2 * F_WIDTH)
    else:
        ab = _radix4(u, batch)
    fo = _seqdft(ab, p["seq_dft"], batch, radix)
    h, xn, eidx, cw, rank, cnt = _postmix(of, ob, r, fo, x2, p["wo"], p["gg"], p["g2"], p["wr"])

    counts = cnt[:, 0].astype(jnp.int32)
    tiles = (counts + TM_EXP - 1) // TM_EXP
    tile_end = jnp.cumsum(tiles)
    tile_start = tile_end - tiles
    experts = jnp.arange(N_EXPERTS, dtype=jnp.int32)
    seg_row = jnp.sum(jnp.where(eidx[:, :, None] == experts, tile_start * TM_EXP, 0), axis=-1)
    pos = seg_row + rank
    n_tiles = 2 * t // TM_EXP + N_EXPERTS
    tile_ids = jnp.arange(n_tiles, dtype=jnp.int32)
    tile_expert = jnp.minimum(jnp.sum((tile_end[None, :] <= tile_ids[:, None]).astype(jnp.int32), axis=1),
                              N_EXPERTS - 1)
    n_active = tile_end[-1:].astype(jnp.int32)
    tail = n_active + jnp.arange(N_EXPERTS, dtype=jnp.int32)
    fill_tiles = jnp.concatenate([jnp.where(tiles > 0, tile_end - 1, -1),
                                  jnp.where(tail < n_tiles, tail, -1)]).astype(jnp.int32)

    xs = _dispatch(xn, pos, fill_tiles, n_tiles * TM_EXP)
    ys = _experts(xs, tile_expert, n_active, p["w_gate"], p["w_up"], p["w_down"])
    y = _combine(ys, pos, cw.T, h, p["gf"])
    return y.reshape(batch, seq, D_MODEL)


def kernel(x_prompt, x_sample, norm1_g, w_in, w_gk2_f, b_gk_f, w_gk2_b, b_gk_b, gla_norm_g, w_out, norm2_g,
           w_group, w_expert, w_gate, w_up, w_down, norm_f_g):
    p = _prepare_params(norm1_g, w_in, w_gk2_f, b_gk_f, w_gk2_b, b_gk_b, gla_norm_g, w_out,
                        norm2_g, w_group, w_expert, w_gate, w_up, w_down, norm_f_g)
    return (_encoder(x_prompt, p), _encoder(x_sample, p))
```

```python
import functools

import numpy as np
import jax
import jax.numpy as jnp
from jax import lax
from jax.experimental import pallas as pl
from jax.experimental.pallas import tpu as pltpu

D_MODEL = 1024
EPS = 1e-6
GLA_HEADS = 4
GLA_DV = 128
GLA_DK = 64
GLA_WIDTH = GLA_HEADS * GLA_DV
GLA_KEY_WIDTH = GLA_HEADS * GLA_DK
GATE_RANK = 16
GATE_NORMALIZER = 16.0
CHUNK = 64
F_GROUPS = 4
F_GROUP_DIM = 128
F_WIDTH = F_GROUPS * F_GROUP_DIM
N_GROUPS = 4
EXPERTS_PER_GROUP = 8
N_EXPERTS = N_GROUPS * EXPERTS_PER_GROUP
D_EXPERT = 256

LANES = 128
V7X_VMEM_BYTES = 64 * 1024 * 1024
VMEM_LIMIT = 56 * 1024 * 1024

TM_TOK = 512
SUB = 128
DFT_N = 2048
DFT_ROWS = 512
TM_EXP = 512
TM_DISP = 512
TM_COMB = 256
ROUTER_ROWS = 48
GATE_COLS = 128
IN_COLS_PAD = 2048 + GATE_COLS
TOK_ROWS = D_MODEL // LANES
N_FILL = 2 * N_EXPERTS

BF16 = jnp.bfloat16
F32 = jnp.float32


def _dot(a, b):
    return jnp.dot(a, b, preferred_element_type=F32)


def _interleave(*stages):
    live = list(stages)
    while live:
        for g in list(live):
            try:
                next(g)
            except StopIteration:
                live.remove(g)


def _dot_nt(a, b):
    return lax.dot_general(a, b, (((1,), (1,)), ((), ())), preferred_element_type=F32)


def _dot_tn(a, b):
    return lax.dot_general(a, b, (((0,), (0,)), ((), ())), preferred_element_type=F32)


def _tri_tables():
    r = np.arange(SUB)
    same = (r[:, None] // CHUNK) == (r[None, :] // CHUNK)
    l_incl = same & (r[None, :] <= r[:, None])
    u_strict = same & (r[None, :] > r[:, None])
    u_incl = same & (r[None, :] >= r[:, None])
    l_strict = same & (r[None, :] < r[:, None])
    fwd = np.concatenate([l_incl, u_strict], 0).astype(np.float32)
    bwd = np.concatenate([u_incl, l_strict], 0).astype(np.float32)
    return fwd, bwd


def _chan_dft_table():
    c = np.arange(F_GROUP_DIM)
    ang = 2.0 * np.pi * ((c[:, None] * c[None, :]) % F_GROUP_DIM) / F_GROUP_DIM
    s = 1.0 / np.sqrt(F_GROUP_DIM)
    return np.concatenate([np.cos(ang) * s, -np.sin(ang) * s], 1).astype(np.float32)


def _seq_dft_table():
    k = np.arange(DFT_N, dtype=np.int64)
    ang = 2.0 * np.pi * ((k[:, None] * k[None, :]) % DFT_N) / DFT_N
    return np.concatenate([np.cos(ang), np.sin(ang)], 1).astype(np.float32)


def _twiddle_tables(radix):
    k1 = np.arange(radix, dtype=np.int64)[:, None]
    s2 = np.arange(DFT_N, dtype=np.int64)[None, :]
    ang = 2.0 * np.pi * ((k1 * s2) % (radix * DFT_N)) / (radix * DFT_N)
    c = np.repeat(np.cos(ang)[:, :, None], LANES, 2).astype(np.float32)
    s = np.repeat(np.sin(ang)[:, :, None], LANES, 2).astype(np.float32)
    return c, s


_TRI_F, _TRI_B = _tri_tables()
_CHAN_DFT = _chan_dft_table()
_SEQ_DFT = _seq_dft_table()


def _inproj_project(x_ref, w_ref, v_ref, r_ref, qk_s, fx_s, gt_s):
    x = x_ref[...]
    inv = lax.rsqrt(jnp.mean(x * x, axis=-1, keepdims=True) + EPS)
    xb = x.astype(BF16)
    yield
    qk_s[...] = _dot(xb, w_ref[:, 0:2 * GLA_KEY_WIDTH]) * inv
    yield
    v_ref[...] = (_dot(xb, w_ref[:, 512:1024]) * inv).astype(BF16)
    yield
    r_ref[...] = (_dot(xb, w_ref[:, 1024:1536]) * inv).astype(BF16)
    yield
    fx_s[...] = (_dot(xb, w_ref[:, 1536:2048]) * inv).astype(BF16)
    yield
    gt_s[...] = (_dot(xb, w_ref[:, 2048:IN_COLS_PAD]) * inv).astype(BF16)


def _inproj_decay(qk_s, fx_s, gt_s, wg_ref, bg_ref, trif_ref, trib_ref, cs_ref,
                  qf_ref, kf_ref, tf_ref, qb_ref, kb_ref, tb_ref, u_ref, decf_ref, decb_ref, tot_ref):
    z = _dot(gt_s[...], wg_ref[...]) + bg_ref[...]
    la = (jnp.minimum(z, 0.0) - jnp.log1p(jnp.exp(-jnp.abs(z)))) * (1.0 / GATE_NORMALIZER)
    la_hi = la.astype(BF16)
    la_lo = (la - la_hi.astype(F32)).astype(BF16)
    trif = trif_ref[...]
    trib = trib_ref[...]
    scale = GLA_DK ** -0.5
    for s in range(TM_TOK // SUB):
        yield
        rows = slice(s * SUB, (s + 1) * SUB)
        q = qk_s[rows, 0:GLA_KEY_WIDTH]
        k = qk_s[rows, GLA_KEY_WIDTH:2 * GLA_KEY_WIDTH]
        rf = _dot(trif, la_hi[rows, 0:GLA_KEY_WIDTH]) + _dot(trif, la_lo[rows, 0:GLA_KEY_WIDTH])
        b, tl = rf[0:SUB], rf[SUB:2 * SUB]
        qf_ref[rows, :] = (q * scale * jnp.exp(b)).astype(BF16)
        kf_ref[rows, :] = (k * jnp.exp(-b)).astype(BF16)
        tf_ref[rows, :] = (k * jnp.exp(tl)).astype(BF16)
        totf = b + tl
        tot_ref[0, rows, :] = totf[:, 0:LANES]
        tot_ref[1, rows, :] = totf[:, LANES:]
        rb = _dot(trib, la_hi[rows, GLA_KEY_WIDTH:]) + _dot(trib, la_lo[rows, GLA_KEY_WIDTH:])
        c, tlb = rb[0:SUB], rb[SUB:2 * SUB]
        qb_ref[rows, :] = (q * scale * jnp.exp(c)).astype(BF16)
        kb_ref[rows, :] = (k * jnp.exp(-c)).astype(BF16)
        tb_ref[rows, :] = (k * jnp.exp(tlb)).astype(BF16)
        totb = c + tlb
        tot_ref[2, rows, :] = totb[:, 0:LANES]
        tot_ref[3, rows, :] = totb[:, LANES:]
    yield
    chunk_rows = pl.ds(0, TM_TOK // CHUNK, stride=CHUNK)
    decf_ref[:, 0:LANES] = jnp.exp(tot_ref[0, chunk_rows, :])
    decf_ref[:, LANES:] = jnp.exp(tot_ref[1, chunk_rows, :])
    decb_ref[:, 0:LANES] = jnp.exp(tot_ref[2, chunk_rows, :])
    decb_ref[:, LANES:] = jnp.exp(tot_ref[3, chunk_rows, :])
    cs = cs_ref[...]
    for g in range(F_GROUPS):
        res = _dot(fx_s[:, g * LANES:(g + 1) * LANES], cs)
        u_ref[:, g * LANES:(g + 1) * LANES] = res[:, 0:LANES].astype(BF16)
        u_ref[:, F_WIDTH + g * LANES:F_WIDTH + (g + 1) * LANES] = res[:, LANES:].astype(BF16)


def _inproj_body(x_ref, w_ref, wg_ref, bg_ref, trif_ref, trib_ref, cs_ref,
                 qf_ref, kf_ref, tf_ref, qb_ref, kb_ref, tb_ref, v_ref, r_ref, u_ref, decf_ref, decb_ref,
                 qk0, qk1, fx0, fx1, gt0, gt1, tot_ref):
    i = pl.program_id(0)

    @pl.when(i == 0)
    def _():
        qk1[...] = jnp.zeros_like(qk1)
        fx1[...] = jnp.zeros_like(fx1)
        gt1[...] = jnp.zeros_like(gt1)

    def step(cur, prev):
        stage_a = _inproj_project(x_ref, w_ref, v_ref, r_ref, *cur)
        stage_b = _inproj_decay(*prev, wg_ref, bg_ref, trif_ref, trib_ref, cs_ref,
                                qf_ref, kf_ref, tf_ref, qb_ref, kb_ref, tb_ref, u_ref, decf_ref, decb_ref, tot_ref)
        _interleave(stage_a, stage_b)

    @pl.when(lax.rem(i, 2) == 0)
    def _():
        step((qk0, fx0, gt0), (qk1, fx1, gt1))

    @pl.when(lax.rem(i, 2) == 1)
    def _():
        step((qk1, fx1, gt1), (qk0, fx0, gt0))


def _inproj(x2, w_in_r, wg, bg, trif, trib, cs):
    t = x2.shape[0]
    nt = t // TM_TOK
    cur = lambda i: (jnp.minimum(i, nt - 1), 0)
    prev = lambda i: (jnp.maximum(i - 1, 0), 0)
    full = lambda a: pl.BlockSpec(a.shape, lambda i: (0,) * a.ndim)
    kw = jax.ShapeDtypeStruct((t, GLA_KEY_WIDTH), BF16)
    dec = jax.ShapeDtypeStruct((t // CHUNK, GLA_KEY_WIDTH), F32)
    dec_spec = pl.BlockSpec((TM_TOK // CHUNK, GLA_KEY_WIDTH), prev)
    wide = jax.ShapeDtypeStruct((t, GLA_WIDTH), BF16)
    return pl.pallas_call(
        _inproj_body,
        grid=(nt + 1,),
        in_specs=[pl.BlockSpec((TM_TOK, D_MODEL), cur), full(w_in_r), full(wg), full(bg),
                  full(trif), full(trib), full(cs)],
        out_specs=[pl.BlockSpec((TM_TOK, GLA_KEY_WIDTH), prev)] * 6
                  + [pl.BlockSpec((TM_TOK, GLA_WIDTH), cur), pl.BlockSpec((TM_TOK, GLA_WIDTH), cur),
                     pl.BlockSpec((TM_TOK, 2 * F_WIDTH), prev), dec_spec, dec_spec],
        out_shape=[kw] * 6 + [wide, wide, jax.ShapeDtypeStruct((t, 2 * F_WIDTH), BF16), dec, dec],
        scratch_shapes=[pltpu.VMEM((TM_TOK, 2 * GLA_KEY_WIDTH), F32)] * 2
                       + [pltpu.VMEM((TM_TOK, F_WIDTH), BF16)] * 2
                       + [pltpu.VMEM((TM_TOK, GATE_COLS), BF16)] * 2
                       + [pltpu.VMEM((2 * GLA_KEY_WIDTH // LANES, TM_TOK, LANES), F32)],
        compiler_params=pltpu.CompilerParams(dimension_semantics=("arbitrary",), vmem_limit_bytes=VMEM_LIMIT),
        name="inproj",
    )(x2, w_in_r, wg, bg, trif, trib, cs)


def _gla_chunk(q_ref, k_ref, t_ref, v_ref, d_ref, o_ref, s_ref, c, causal, m_lo, mv_lo, bd):
    rows = slice(c * CHUNK, (c + 1) * CHUNK)
    dec = d_ref[c:c + 1, :]
    for p in range(GLA_HEADS // 2):
        kl = slice(p * LANES, (p + 1) * LANES)
        vl = slice(p * 2 * GLA_DV, (p + 1) * 2 * GLA_DV)
        qd = q_ref[rows, kl]
        kd = k_ref[rows, kl]
        kt = t_ref[rows, kl]
        vv = v_ref[rows, vl]
        zk = jnp.zeros_like(kd)
        zv = jnp.zeros_like(vv)
        kbd = jnp.concatenate([jnp.where(m_lo, kd, zk), jnp.where(m_lo, zk, kd)], axis=0)
        att = _dot_nt(qd, kbd)
        att = jnp.where(causal, att, 0.0).astype(BF16)
        vbd = jnp.concatenate([jnp.where(mv_lo, vv, zv), jnp.where(mv_lo, zv, vv)], axis=0)
        st = s_ref[p]
        o_ref[rows, vl] = (_dot(att, vbd) + _dot_nt(qd, st.astype(BF16))).astype(o_ref.dtype)
        kv = _dot_tn(vv, kt)
        s_ref[p] = st * dec[:, kl] + jnp.where(bd, kv, 0.0)


def _gla_body(qf_ref, kf_ref, tf_ref, vf_ref, df_ref, qb_ref, kb_ref, tb_ref, vb_ref, db_ref,
              of_ref, ob_ref, sf_ref, sb_ref):
    @pl.when(pl.program_id(1) == 0)
    def _():
        sf_ref[...] = jnp.zeros_like(sf_ref)
        sb_ref[...] = jnp.zeros_like(sb_ref)

    lane = lax.broadcasted_iota(jnp.int32, (CHUNK, LANES), 1)
    row = lax.broadcasted_iota(jnp.int32, (CHUNK, LANES), 0)
    m_lo = lane < GLA_DK
    col = lane & (CHUNK - 1)
    causal_f = row >= col
    causal_b = row <= col
    mv_lo = lax.broadcasted_iota(jnp.int32, (CHUNK, 2 * GLA_DV), 1) < GLA_DV
    bd = ((lax.broadcasted_iota(jnp.int32, (2 * GLA_DV, LANES), 0) < GLA_DV)
          == (lax.broadcasted_iota(jnp.int32, (2 * GLA_DV, LANES), 1) < GLA_DK))
    n = TM_TOK // CHUNK
    for j in range(n):
        _gla_chunk(qf_ref, kf_ref, tf_ref, vf_ref, df_ref, of_ref, sf_ref, j, causal_f, m_lo, mv_lo, bd)
        _gla_chunk(qb_ref, kb_ref, tb_ref, vb_ref, db_ref, ob_ref, sb_ref, n - 1 - j, causal_b, m_lo, mv_lo, bd)


def _gla(qf, kf, tf, qb, kb, tb, v, decf, decb, batch, seq):
    t = batch * seq
    nt = seq // TM_TOK
    fwd = lambda b, i: (b * nt + i, 0)
    bwd = lambda b, i: (b * nt + nt - 1 - i, 0)
    ks = lambda m: pl.BlockSpec((TM_TOK, GLA_KEY_WIDTH), m)
    vs = lambda m: pl.BlockSpec((TM_TOK, GLA_WIDTH), m)
    ds = lambda m: pl.BlockSpec((TM_TOK // CHUNK, GLA_KEY_WIDTH), m)
    o = jax.ShapeDtypeStruct((t, GLA_WIDTH), BF16)
    state = pltpu.VMEM((GLA_HEADS // 2, 2 * GLA_DV, LANES), F32)
    return pl.pallas_call(
        _gla_body,
        grid=(batch, nt),
        in_specs=[ks(fwd), ks(fwd), ks(fwd), vs(fwd), ds(fwd), ks(bwd), ks(bwd), ks(bwd), vs(bwd), ds(bwd)],
        out_specs=[vs(fwd), vs(bwd)],
        out_shape=[o, o],
        scratch_shapes=[state, state],
        compiler_params=pltpu.CompilerParams(dimension_semantics=("arbitrary", "arbitrary"),
                                             vmem_limit_bytes=VMEM_LIMIT),
        name="gla",
    )(qf, kf, tf, v, decf, qb, kb, tb, v, decb)


RADIX_ROWS = 256


def _radix4_body(z_ref, twc_ref, tws_ref, y_ref):
    z = [z_ref[s].astype(F32) for s in range(4)]
    re = [a[:, 0:F_WIDTH] for a in z]
    im = [a[:, F_WIDTH:] for a in z]
    ar, ai = re[0] + re[2], im[0] + im[2]
    br, bi = re[0] - re[2], im[0] - im[2]
    cr, ci = re[1] + re[3], im[1] + im[3]
    dr, di = re[1] - re[3], im[1] - im[3]
    y = [(ar + cr, ai + ci), (br + di, bi - dr), (ar - cr, ai - ci), (br - di, bi + dr)]
    y_ref[0, :, 0:F_WIDTH] = y[0][0].astype(BF16)
    y_ref[0, :, F_WIDTH:] = y[0][1].astype(BF16)
    for k1 in range(1, 4):
        c = jnp.concatenate([twc_ref[k1]] * (F_WIDTH // LANES), axis=1)
        s = jnp.concatenate([tws_ref[k1]] * (F_WIDTH // LANES), axis=1)
        yr, yi = y[k1]
        y_ref[k1, :, 0:F_WIDTH] = (yr * c + yi * s).astype(BF16)
        y_ref[k1, :, F_WIDTH:] = (yi * c - yr * s).astype(BF16)


def _radix4(u, batch):
    z = u.reshape(batch, 4, DFT_N, 2 * F_WIDTH)
    twc, tws = _twiddle_tables(4)
    twc, tws = jnp.asarray(twc), jnp.asarray(tws)
    nr = DFT_N // RADIX_ROWS
    blk = pl.BlockSpec((None, 4, RADIX_ROWS, 2 * F_WIDTH), lambda b, i: (b, 0, i, 0))
    tw = pl.BlockSpec((4, RADIX_ROWS, LANES), lambda b, i: (0, i, 0))
    y = pl.pallas_call(
        _radix4_body,
        grid=(batch, nr),
        in_specs=[blk, tw, tw],
        out_specs=blk,
        out_shape=jax.ShapeDtypeStruct((batch, 4, DFT_N, 2 * F_WIDTH), BF16),
        compiler_params=pltpu.CompilerParams(dimension_semantics=("arbitrary", "arbitrary"),
                                             vmem_limit_bytes=VMEM_LIMIT),
        name="radix4",
    )(z, twc, tws)
    return y


def _seqdft_body(cs_ref, ab_ref, o_ref, *, radix, scale):
    for k1 in range(radix):
        a = ab_ref[k1, :, 0:F_WIDTH]
        b = ab_ref[k1, :, F_WIDTH:]
        res = (_dot(cs_ref[:, 0:DFT_N], a) + _dot(cs_ref[:, DFT_N:], b)) * scale
        for c in range(F_WIDTH // LANES):
            o_ref[c, pl.ds(k1, DFT_ROWS, stride=radix), :] = res[:, c * LANES:(c + 1) * LANES]


def _seqdft(ab, cs, batch, radix):
    scale = float(1.0 / np.sqrt(radix * DFT_N))
    nj = DFT_N // DFT_ROWS
    return pl.pallas_call(
        functools.partial(_seqdft_body, radix=radix, scale=scale),
        grid=(batch, nj),
        in_specs=[pl.BlockSpec((DFT_ROWS, 2 * DFT_N), lambda b, j: (j, 0)),
                  pl.BlockSpec((None, radix, DFT_N, 2 * F_WIDTH), lambda b, j: (b, 0, 0, 0))],
        out_specs=pl.BlockSpec((F_WIDTH // LANES, radix * DFT_ROWS, LANES), lambda b, j: (0, b * nj + j, 0)),
        out_shape=jax.ShapeDtypeStruct((F_WIDTH // LANES, batch * radix * DFT_N, LANES), F32),
        compiler_params=pltpu.CompilerParams(dimension_semantics=("arbitrary", "arbitrary"),
                                             vmem_limit_bytes=VMEM_LIMIT),
        name="seqdft",
    )(cs, ab)


def _first_index(hit, rows):
    return jnp.min(jnp.where(hit, rows.astype(F32), 1e6), axis=0, keepdims=True).astype(jnp.int32)


def _postmix_body(of_ref, ob_ref, r_ref, fo_ref, x_ref, wo_ref, gg_ref, g2_ref, wr_ref,
                  h_ref, xn_ref, eidx_ref, cw_ref, rank_ref, cnt_ref, carry_ref):
    @pl.when(pl.program_id(0) == 0)
    def _():
        carry_ref[...] = jnp.zeros_like(carry_ref)

    o = of_ref[...].astype(F32) + ob_ref[...].astype(F32)
    r = r_ref[...].astype(F32)
    parts = []
    for hd in range(GLA_HEADS):
        sl = slice(hd * GLA_DV, (hd + 1) * GLA_DV)
        oh = o[:, sl]
        oh = oh * lax.rsqrt(jnp.mean(oh * oh, axis=-1, keepdims=True) + EPS)
        rh = r[:, sl]
        parts.append((oh * gg_ref[...] * (rh * jax.nn.sigmoid(rh))).astype(BF16))
    on = jnp.concatenate(parts, axis=1)
    fo = jnp.concatenate([fo_ref[c] for c in range(F_WIDTH // LANES)], axis=1)
    mixed = _dot(on, wo_ref[0:GLA_WIDTH, :]) + _dot(fo.astype(BF16), wo_ref[GLA_WIDTH:, :])
    h = x_ref[...] + mixed
    h_ref[...] = h
    xn = h * lax.rsqrt(jnp.mean(h * h, axis=-1, keepdims=True) + EPS) * g2_ref[...]
    _store_token_tiles(xn_ref, xn)
    logits = _dot_nt(wr_ref[...], xn.astype(BF16))

    sub8 = lax.broadcasted_iota(jnp.int32, (8, TM_TOK), 0)
    lg = jnp.where(sub8 < N_GROUPS, logits[0:8], -jnp.inf)
    gmax = jnp.max(lg, axis=0, keepdims=True)
    g_w = 1.0 / jnp.sum(jnp.exp(lg - gmax), axis=0, keepdims=True)
    g_sel = _first_index(lg == gmax, sub8)
    sel = logits[8:16]
    for g in range(1, N_GROUPS):
        sel = jnp.where(g_sel == g, logits[8 + 8 * g:16 + 8 * g], sel)
    m1 = jnp.max(sel, axis=0, keepdims=True)
    i1 = _first_index(sel == m1, sub8)
    sel2 = jnp.where(sub8 == i1, -jnp.inf, sel)
    m2 = jnp.max(sel2, axis=0, keepdims=True)
    i2 = _first_index(sel2 == m2, sub8)
    e21 = jnp.exp(m2 - m1)
    w1 = 1.0 / (1.0 + e21)
    w2 = e21 / (1.0 + e21)
    e1 = g_sel * EXPERTS_PER_GROUP + i1
    e2 = g_sel * EXPERTS_PER_GROUP + i2
    eidx_ref[...] = jnp.concatenate([e1, e2], axis=0)
    cw_ref[...] = jnp.concatenate([g_w * w1, g_w * w2], axis=0)

    sub = lax.broadcasted_iota(jnp.int32, (N_EXPERTS, TM_TOK), 0)
    oh1 = sub == e1
    oh2 = sub == e2
    oh1b = jnp.where(oh1, 1.0, 0.0).astype(BF16)
    oh2b = jnp.where(oh2, 1.0, 0.0).astype(BF16)
    before = (lax.broadcasted_iota(jnp.int32, (TM_TOK, TM_TOK), 0)
              < lax.broadcasted_iota(jnp.int32, (TM_TOK, TM_TOK), 1))
    before = jnp.where(before, 1.0, 0.0).astype(BF16)
    ones = jnp.ones((TM_TOK, LANES), BF16)
    p1 = _dot(oh1b, before)
    p2 = _dot(oh2b, before)
    c1 = _dot(oh1b, ones)
    c2 = _dot(oh2b, ones)
    carry = carry_ref[...]
    rep = TM_TOK // LANES
    base1 = jnp.concatenate([carry] * rep, axis=1)
    base2 = jnp.concatenate([carry + c1] * rep, axis=1)
    rk1 = jnp.sum(jnp.where(oh1, p1 + base1, 0.0), axis=0, keepdims=True)
    rk2 = jnp.sum(jnp.where(oh2, p2 + base2, 0.0), axis=0, keepdims=True)
    rank_ref[...] = jnp.concatenate([rk1, rk2], axis=0).astype(jnp.int32)
    carry = carry + c1 + c2
    carry_ref[...] = carry
    cnt_ref[...] = carry


def _postmix(of, ob, r, fo, x2, wo, gg, g2, wr):
    t = x2.shape[0]
    nt = t // TM_TOK
    tok = lambda w: pl.BlockSpec((TM_TOK, w), lambda i: (i, 0))
    full = lambda a: pl.BlockSpec(a.shape, lambda i: (0,) * a.ndim)
    lane2 = pl.BlockSpec((2, TM_TOK), lambda i: (0, i))
    return pl.pallas_call(
        _postmix_body,
        grid=(nt,),
        in_specs=[tok(GLA_WIDTH), tok(GLA_WIDTH), tok(GLA_WIDTH),
                  pl.BlockSpec((F_WIDTH // LANES, TM_TOK, LANES), lambda i: (0, i, 0)), tok(D_MODEL),
                  full(wo), full(gg), full(g2), full(wr)],
        out_specs=[tok(D_MODEL), pl.BlockSpec((TM_TOK * TOK_ROWS, LANES), lambda i: (i, 0)), lane2, lane2, lane2,
                   pl.BlockSpec((N_EXPERTS, LANES), lambda i: (0, 0))],
        out_shape=[jax.ShapeDtypeStruct((t, D_MODEL), F32), jax.ShapeDtypeStruct((t * TOK_ROWS, LANES), F32),
                   jax.ShapeDtypeStruct((2, t), jnp.int32), jax.ShapeDtypeStruct((2, t), F32),
                   jax.ShapeDtypeStruct((2, t), jnp.int32), jax.ShapeDtypeStruct((N_EXPERTS, LANES), F32)],
        scratch_shapes=[pltpu.VMEM((N_EXPERTS, LANES), F32)],
        compiler_params=pltpu.CompilerParams(dimension_semantics=("arbitrary",), vmem_limit_bytes=VMEM_LIMIT),
        name="postmix",
    )(of, ob, r, fo, x2, wo, gg, g2, wr)


def _store_token_tiles(ref, val):
    n = val.shape[0]
    for c in range(TOK_ROWS):
        ref[pl.ds(c, n, stride=TOK_ROWS), :] = val[:, c * LANES:(c + 1) * LANES]


def _load_token_tiles(ref, n):
    return jnp.concatenate([ref[pl.ds(c, n, stride=TOK_ROWS), :] for c in range(TOK_ROWS)], axis=1)


def _token_rows(tok):
    if isinstance(tok, int):
        return pl.ds(tok * TOK_ROWS, TOK_ROWS)
    return pl.ds(pl.multiple_of(tok * TOK_ROWS, TOK_ROWS), TOK_ROWS)


def _row_copy(src_hbm, dst_hbm, src_tok, dst_tok, sem):
    return pltpu.make_async_copy(src_hbm.at[_token_rows(src_tok)], dst_hbm.at[_token_rows(dst_tok)], sem)


def _dispatch_body(fill_ref, pos_ref, xn_ref, xs_hbm, zeros_ref, fill_sem, row_sem):
    i = pl.program_id(0)

    @pl.when(i == 0)
    def _():
        zeros_ref[...] = jnp.zeros_like(zeros_ref)
        tile_rows = TM_EXP * TOK_ROWS

        def fill(e):
            rows = pl.ds(pl.multiple_of(fill_ref[e] * tile_rows, tile_rows), tile_rows)
            return pltpu.make_async_copy(zeros_ref, xs_hbm.at[rows], fill_sem)

        for e in range(N_FILL):
            @pl.when(fill_ref[e] >= 0)
            def _():
                fill(e).start()
        for e in range(N_FILL):
            @pl.when(fill_ref[e] >= 0)
            def _():
                fill(e).wait()

    for j in range(TM_DISP):
        for s in range(2):
            _row_copy(xn_ref, xs_hbm, j, pos_ref[s, j], row_sem).start(priority=s)
    for j in range(TM_DISP):
        for s in range(2):
            _row_copy(xn_ref, xs_hbm, j, 0, row_sem).wait()


def _dispatch(xn, pos, fill_rows, rows_total):
    t = xn.shape[0] // TOK_ROWS
    grid_spec = pltpu.PrefetchScalarGridSpec(
        num_scalar_prefetch=1,
        grid=(t // TM_DISP,),
        in_specs=[pl.BlockSpec((2, TM_DISP), lambda i, fill: (0, i), memory_space=pltpu.SMEM),
                  pl.BlockSpec((TM_DISP * TOK_ROWS, LANES), lambda i, fill: (i, 0))],
        out_specs=pl.BlockSpec(memory_space=pl.ANY),
        scratch_shapes=[pltpu.VMEM((TM_EXP * TOK_ROWS, LANES), F32), pltpu.SemaphoreType.DMA(()),
                        pltpu.SemaphoreType.DMA(())],
    )
    return pl.pallas_call(
        _dispatch_body,
        grid_spec=grid_spec,
        out_shape=jax.ShapeDtypeStruct((rows_total * TOK_ROWS, LANES), F32),
        compiler_params=pltpu.CompilerParams(dimension_semantics=("arbitrary",), vmem_limit_bytes=VMEM_LIMIT),
        name="dispatch",
    )(fill_rows, pos, xn)


def _experts_body(te_ref, na_ref, xs_ref, wg_ref, wu_ref, wd_ref, ys_ref):
    @pl.when(pl.program_id(0) < na_ref[0])
    def _():
        x = _load_token_tiles(xs_ref, TM_EXP).astype(BF16)
        gate = _dot(x, wg_ref[...])
        up = _dot(x, wu_ref[...])
        hid = (gate * jax.nn.sigmoid(gate) * up).astype(BF16)
        _store_token_tiles(ys_ref, _dot(hid, wd_ref[...]))

    @pl.when(pl.program_id(0) >= na_ref[0])
    def _():
        ys_ref[...] = jnp.zeros_like(ys_ref)


def _experts(xs, tile_expert, n_active, wg, wu, wd):
    n_tiles = xs.shape[0] // (TM_EXP * TOK_ROWS)
    row = lambda i, te, na: (i, 0)
    wsel = lambda i, te, na: (te[jnp.minimum(i, na[0] - 1)], 0, 0)
    grid_spec = pltpu.PrefetchScalarGridSpec(
        num_scalar_prefetch=2,
        grid=(n_tiles,),
        in_specs=[pl.BlockSpec((TM_EXP * TOK_ROWS, LANES), row),
                  pl.BlockSpec((None, D_MODEL, D_EXPERT), wsel),
                  pl.BlockSpec((None, D_MODEL, D_EXPERT), wsel),
                  pl.BlockSpec((None, D_EXPERT, D_MODEL), wsel)],
        out_specs=pl.BlockSpec((TM_EXP * TOK_ROWS, LANES), row),
    )
    return pl.pallas_call(
        _experts_body,
        grid_spec=grid_spec,
        out_shape=jax.ShapeDtypeStruct(xs.shape, F32),
        compiler_params=pltpu.CompilerParams(dimension_semantics=("arbitrary",), vmem_limit_bytes=VMEM_LIMIT),
        name="experts",
    )(tile_expert, n_active, xs, wg, wu, wd)


def _combine_body(pos_ref, nxt_ref, ys_hbm, cw_ref, h_ref, gf_ref, y_ref, buf_ref, sem):
    i = pl.program_id(0)
    slot = lax.rem(i, 2)

    def row_gather(p_ref, sl, s, j):
        return pltpu.make_async_copy(ys_hbm.at[_token_rows(p_ref[s, j])], buf_ref.at[sl, s, _token_rows(j)],
                                     sem.at[sl])

    def issue_tile(p_ref, sl):
        for j in range(TM_COMB):
            for s in range(2):
                row_gather(p_ref, sl, s, j).start(priority=s)

    @pl.when(i == 0)
    def _():
        issue_tile(pos_ref, slot)

    @pl.when(i + 1 < pl.num_programs(0))
    def _():
        issue_tile(nxt_ref, 1 - slot)

    for j in range(TM_COMB):
        for s in range(2):
            pltpu.make_async_copy(ys_hbm.at[_token_rows(0)], buf_ref.at[slot, s, _token_rows(j)], sem.at[slot]).wait()

    cw = cw_ref[...]
    y0 = _load_token_tiles(buf_ref.at[slot, 0], TM_COMB)
    y1 = _load_token_tiles(buf_ref.at[slot, 1], TM_COMB)
    h = h_ref[...] + cw[:, 0:1] * y0 + cw[:, 1:2] * y1
    y_ref[...] = h * lax.rsqrt(jnp.mean(h * h, axis=-1, keepdims=True) + EPS) * gf_ref[...]


def _combine(ys, pos, cw_t, h, gf):
    t = h.shape[0]
    n = t // TM_COMB
    tok = pl.BlockSpec((TM_COMB, D_MODEL), lambda i: (i, 0))
    return pl.pallas_call(
        _combine_body,
        grid=(n,),
        in_specs=[pl.BlockSpec((2, TM_COMB), lambda i: (0, i), memory_space=pltpu.SMEM),
                  pl.BlockSpec((2, TM_COMB), lambda i: (0, jnp.minimum(i + 1, n - 1)), memory_space=pltpu.SMEM),
                  pl.BlockSpec(memory_space=pl.ANY),
                  pl.BlockSpec((TM_COMB, 2), lambda i: (i, 0)),
                  tok,
                  pl.BlockSpec((1, D_MODEL), lambda i: (0, 0))],
        out_specs=tok,
        out_shape=jax.ShapeDtypeStruct((t, D_MODEL), F32),
        scratch_shapes=[pltpu.VMEM((2, 2, TM_COMB * TOK_ROWS, LANES), F32), pltpu.SemaphoreType.DMA((2,))],
        compiler_params=pltpu.CompilerParams(dimension_semantics=("arbitrary",), vmem_limit_bytes=VMEM_LIMIT),
        name="combine",
    )(pos, pos, ys, cw_t, h, gf)


def _prepare_params(norm1_g, w_in, w_gk2_f, b_gk_f, w_gk2_b, b_gk_b, gla_norm_g, w_out,
                    norm2_g, w_group, w_expert, w_gate, w_up, w_down, norm_f_g):
    w = w_in[0] * norm1_g[0][:, None]
    gate_lo = 2 * GLA_KEY_WIDTH + 2 * GLA_WIDTH
    gate_hi = gate_lo + 2 * GATE_RANK
    w_in_r = jnp.concatenate([w[:, :gate_lo], w[:, gate_hi:], w[:, gate_lo:gate_hi],
                              jnp.zeros((D_MODEL, GATE_COLS - 2 * GATE_RANK), F32)], axis=1).astype(BF16)
    zk = jnp.zeros((GATE_RANK, GLA_KEY_WIDTH), F32)
    wg = jnp.concatenate([jnp.concatenate([w_gk2_f[0], zk], axis=1), jnp.concatenate([zk, w_gk2_b[0]], axis=1),
                          jnp.zeros((GATE_COLS - 2 * GATE_RANK, 2 * GLA_KEY_WIDTH), F32)], axis=0).astype(BF16)
    bg = jnp.concatenate([b_gk_f[0], b_gk_b[0]])[None, :]
    wr = jnp.concatenate([w_group[0].T, jnp.zeros((8 - N_GROUPS, D_MODEL), F32), w_expert[0].T,
                          jnp.zeros((ROUTER_ROWS - 8 - N_EXPERTS, D_MODEL), F32)], axis=0).astype(BF16)
    return dict(
        w_in_r=w_in_r, wg=wg, bg=bg,
        gg=gla_norm_g[0][None, :], wo=w_out[0].astype(BF16), g2=norm2_g[0][None, :], wr=wr,
        w_gate=w_gate[0].astype(BF16), w_up=w_up[0].astype(BF16), w_down=w_down[0].astype(BF16),
        gf=norm_f_g[None, :],
        trif=jnp.asarray(_TRI_F).astype(BF16), trib=jnp.asarray(_TRI_B).astype(BF16),
        chan_dft=jnp.asarray(_CHAN_DFT).astype(BF16), seq_dft=jnp.asarray(_SEQ_DFT).astype(BF16))


def _encoder(x, p):
    batch, seq, _ = x.shape
    t = batch * seq
    assert seq % DFT_N == 0 and seq // DFT_N in (1, 4), "sequence DFT supports seq = 2048 or 8192"
    radix = seq // DFT_N
    x2 = x.reshape(t, D_MODEL)
    qf, kf, tf, qb, kb, tb, v, r, u, decf, decb = _inproj(x2, p["w_in_r"], p["wg"], p["bg"],
                                                          p["trif"], p["trib"], p["chan_dft"])
    of, ob = _gla(qf, kf, tf, qb, kb, tb, v, decf, decb, batch, seq)
    if radix == 1:
        ab = u.reshape(batch, 1, DFT_N, 2 * F_WIDTH)
    else:
        ab = _radix4(u, batch)
    fo = _seqdft(ab, p["seq_dft"], batch, radix)
    h, xn, eidx, cw, rank, cnt = _postmix(of, ob, r, fo, x2, p["wo"], p["gg"], p["g2"], p["wr"])

    counts = cnt[:, 0].astype(jnp.int32)
    tiles = (counts + TM_EXP - 1) // TM_EXP
    tile_end = jnp.cumsum(tiles)
    tile_start = tile_end - tiles
    experts = jnp.arange(N_EXPERTS, dtype=jnp.int32)
    seg_row = jnp.sum(jnp.where(eidx[:, :, None] == experts, tile_start * TM_EXP, 0), axis=-1)
    pos = seg_row + rank
    n_tiles = 2 * t // TM_EXP + N_EXPERTS
    tile_ids = jnp.arange(n_tiles, dtype=jnp.int32)
    tile_expert = jnp.minimum(jnp.sum((tile_end[None, :] <= tile_ids[:, None]).astype(jnp.int32), axis=1),
                              N_EXPERTS - 1)
    n_active = tile_end[-1:].astype(jnp.int32)
    tail = n_active + jnp.arange(N_EXPERTS, dtype=jnp.int32)
    fill_tiles = jnp.concatenate([jnp.where(tiles > 0, tile_end - 1, -1),
                                  jnp.where(tail < n_tiles, tail, -1)]).astype(jnp.int32)

    xs = _dispatch(xn, pos, fill_tiles, n_tiles * TM_EXP)
    ys = _experts(xs, tile_expert, n_active, p["w_gate"], p["w_up"], p["w_down"])
    y = _combine(ys, pos, cw.T, h, p["gf"])
    return y.reshape(batch, seq, D_MODEL)


def kernel(x_prompt, x_sample, norm1_g, w_in, w_gk2_f, b_gk_f, w_gk2_b, b_gk_b, gla_norm_g, w_out, norm2_g,
           w_group, w_expert, w_gate, w_up, w_down, norm_f_g):
    p = _prepare_params(norm1_g, w_in, w_gk2_f, b_gk_f, w_gk2_b, b_gk_b, gla_norm_g, w_out,
                        norm2_g, w_group, w_expert, w_gate, w_up, w_down, norm_f_g)
    return (_encoder(x_prompt, p), _encoder(x_sample, p))
```

```python
import functools

import numpy as np
import jax
import jax.numpy as jnp
from jax import lax
from jax.experimental import pallas as pl
from jax.experimental.pallas import tpu as pltpu

D_MODEL = 1024
EPS = 1e-6
GLA_HEADS = 4
GLA_DV = 128
GLA_DK = 64
GLA_WIDTH = GLA_HEADS * GLA_DV
GLA_KEY_WIDTH = GLA_HEADS * GLA_DK
GATE_RANK = 16
GATE_NORMALIZER = 16.0
CHUNK = 64
F_GROUPS = 4
F_GROUP_DIM = 128
F_WIDTH = F_GROUPS * F_GROUP_DIM
N_GROUPS = 4
EXPERTS_PER_GROUP = 8
N_EXPERTS = N_GROUPS * EXPERTS_PER_GROUP
D_EXPERT = 256

LANES = 128
V7X_VMEM_BYTES = 64 * 1024 * 1024
VMEM_LIMIT = 56 * 1024 * 1024

TM_TOK = 512
SUB = 128
DFT_N = 2048
DFT_ROWS = 512
TM_EXP = 512
TM_DISP = 512
TM_COMB = 512
ROUTER_ROWS = 48
GATE_COLS = 128
IN_COLS_PAD = 2048 + GATE_COLS
TOK_ROWS = D_MODEL // LANES
N_FILL = 2 * N_EXPERTS

BF16 = jnp.bfloat16
F32 = jnp.float32


def _dot(a, b):
    return jnp.dot(a, b, preferred_element_type=F32)


def _interleave(*stages):
    live = list(stages)
    while live:
        for g in list(live):
            try:
                next(g)
            except StopIteration:
                live.remove(g)


def _dot_nt(a, b):
    return lax.dot_general(a, b, (((1,), (1,)), ((), ())), preferred_element_type=F32)


def _dot_tn(a, b):
    return lax.dot_general(a, b, (((0,), (0,)), ((), ())), preferred_element_type=F32)


def _tri_tables():
    r = np.arange(SUB)
    same = (r[:, None] // CHUNK) == (r[None, :] // CHUNK)
    l_incl = same & (r[None, :] <= r[:, None])
    u_strict = same & (r[None, :] > r[:, None])
    u_incl = same & (r[None, :] >= r[:, None])
    l_strict = same & (r[None, :] < r[:, None])
    fwd = np.concatenate([l_incl, u_strict], 0).astype(np.float32)
    bwd = np.concatenate([u_incl, l_strict], 0).astype(np.float32)
    return fwd, bwd


def _chan_dft_table():
    c = np.arange(F_GROUP_DIM)
    ang = 2.0 * np.pi * ((c[:, None] * c[None, :]) % F_GROUP_DIM) / F_GROUP_DIM
    s = 1.0 / np.sqrt(F_GROUP_DIM)
    return np.concatenate([np.cos(ang) * s, -np.sin(ang) * s], 1).astype(np.float32)


def _seq_dft_table():
    k = np.arange(DFT_N, dtype=np.int64)
    ang = 2.0 * np.pi * ((k[:, None] * k[None, :]) % DFT_N) / DFT_N
    return np.concatenate([np.cos(ang), np.sin(ang)], 1).astype(np.float32)


def _twiddle_tables(radix):
    k1 = np.arange(radix, dtype=np.int64)[:, None]
    s2 = np.arange(DFT_N, dtype=np.int64)[None, :]
    ang = 2.0 * np.pi * ((k1 * s2) % (radix * DFT_N)) / (radix * DFT_N)
    c = np.repeat(np.cos(ang)[:, :, None], LANES, 2).astype(np.float32)
    s = np.repeat(np.sin(ang)[:, :, None], LANES, 2).astype(np.float32)
    return c, s


_TRI_F, _TRI_B = _tri_tables()
_CHAN_DFT = _chan_dft_table()
_SEQ_DFT = _seq_dft_table()


def _inproj_project(x_ref, w_ref, v_ref, r_ref, qk_s, fx_s, gt_s):
    x = x_ref[...]
    inv = lax.rsqrt(jnp.mean(x * x, axis=-1, keepdims=True) + EPS)
    xb = x.astype(BF16)
    yield
    qk_s[...] = _dot(xb, w_ref[:, 0:2 * GLA_KEY_WIDTH]) * inv
    yield
    v_ref[...] = (_dot(xb, w_ref[:, 512:1024]) * inv).astype(BF16)
    yield
    r_ref[...] = (_dot(xb, w_ref[:, 1024:1536]) * inv).astype(BF16)
    yield
    fx_s[...] = (_dot(xb, w_ref[:, 1536:2048]) * inv).astype(BF16)
    yield
    gt_s[...] = (_dot(xb, w_ref[:, 2048:IN_COLS_PAD]) * inv).astype(BF16)


def _inproj_decay(qk_s, fx_s, gt_s, wg_ref, bg_ref, trif_ref, trib_ref, cs_ref,
                  qf_ref, kf_ref, tf_ref, qb_ref, kb_ref, tb_ref, u_ref, decf_ref, decb_ref, tot_ref):
    z = _dot(gt_s[...], wg_ref[...]) + bg_ref[...]
    la = (jnp.minimum(z, 0.0) - jnp.log1p(jnp.exp(-jnp.abs(z)))) * (1.0 / GATE_NORMALIZER)
    la_hi = la.astype(BF16)
    la_lo = (la - la_hi.astype(F32)).astype(BF16)
    trif = trif_ref[...]
    trib = trib_ref[...]
    scale = GLA_DK ** -0.5
    for s in range(TM_TOK // SUB):
        yield
        rows = slice(s * SUB, (s + 1) * SUB)
        q = qk_s[rows, 0:GLA_KEY_WIDTH]
        k = qk_s[rows, GLA_KEY_WIDTH:2 * GLA_KEY_WIDTH]
        rf = _dot(trif, la_hi[rows, 0:GLA_KEY_WIDTH]) + _dot(trif, la_lo[rows, 0:GLA_KEY_WIDTH])
        b, tl = rf[0:SUB], rf[SUB:2 * SUB]
        qf_ref[rows, :] = (q * scale * jnp.exp(b)).astype(BF16)
        kf_ref[rows, :] = (k * jnp.exp(-b)).astype(BF16)
        tf_ref[rows, :] = (k * jnp.exp(tl)).astype(BF16)
        totf = b + tl
        tot_ref[0, rows, :] = totf[:, 0:LANES]
        tot_ref[1, rows, :] = totf[:, LANES:]
        rb = _dot(trib, la_hi[rows, GLA_KEY_WIDTH:]) + _dot(trib, la_lo[rows, GLA_KEY_WIDTH:])
        c, tlb = rb[0:SUB], rb[SUB:2 * SUB]
        qb_ref[rows, :] = (q * scale * jnp.exp(c)).astype(BF16)
        kb_ref[rows, :] = (k * jnp.exp(-c)).astype(BF16)
        tb_ref[rows, :] = (k * jnp.exp(tlb)).astype(BF16)
        totb = c + tlb
        tot_ref[2, rows, :] = totb[:, 0:LANES]
        tot_ref[3, rows, :] = totb[:, LANES:]
    yield
    chunk_rows = pl.ds(0, TM_TOK // CHUNK, stride=CHUNK)
    decf_ref[:, 0:LANES] = jnp.exp(tot_ref[0, chunk_rows, :])
    decf_ref[:, LANES:] = jnp.exp(tot_ref[1, chunk_rows, :])
    decb_ref[:, 0:LANES] = jnp.exp(tot_ref[2, chunk_rows, :])
    decb_ref[:, LANES:] = jnp.exp(tot_ref[3, chunk_rows, :])
    cs = cs_ref[...]
    for g in range(F_GROUPS):
        res = _dot(fx_s[:, g * LANES:(g + 1) * LANES], cs)
        u_ref[:, g * LANES:(g + 1) * LANES] = res[:, 0:LANES].astype(BF16)
        u_ref[:, F_WIDTH + g * LANES:F_WIDTH + (g + 1) * LANES] = res[:, LANES:].astype(BF16)


def _inproj_body(x_ref, w_ref, wg_ref, bg_ref, trif_ref, trib_ref, cs_ref,
                 qf_ref, kf_ref, tf_ref, qb_ref, kb_ref, tb_ref, v_ref, r_ref, u_ref, decf_ref, decb_ref,
                 qk0, qk1, fx0, fx1, gt0, gt1, tot_ref):
    i = pl.program_id(0)

    @pl.when(i == 0)
    def _():
        qk1[...] = jnp.zeros_like(qk1)
        fx1[...] = jnp.zeros_like(fx1)
        gt1[...] = jnp.zeros_like(gt1)

    def step(cur, prev):
        stage_a = _inproj_project(x_ref, w_ref, v_ref, r_ref, *cur)
        stage_b = _inproj_decay(*prev, wg_ref, bg_ref, trif_ref, trib_ref, cs_ref,
                                qf_ref, kf_ref, tf_ref, qb_ref, kb_ref, tb_ref, u_ref, decf_ref, decb_ref, tot_ref)
        _interleave(stage_a, stage_b)

    @pl.when(lax.rem(i, 2) == 0)
    def _():
        step((qk0, fx0, gt0), (qk1, fx1, gt1))

    @pl.when(lax.rem(i, 2) == 1)
    def _():
        step((qk1, fx1, gt1), (qk0, fx0, gt0))


def _inproj(x2, w_in_r, wg, bg, trif, trib, cs):
    t = x2.shape[0]
    nt = t // TM_TOK
    cur = lambda i: (jnp.minimum(i, nt - 1), 0)
    prev = lambda i: (jnp.maximum(i - 1, 0), 0)
    full = lambda a: pl.BlockSpec(a.shape, lambda i: (0,) * a.ndim)
    kw = jax.ShapeDtypeStruct((t, GLA_KEY_WIDTH), BF16)
    dec = jax.ShapeDtypeStruct((t // CHUNK, GLA_KEY_WIDTH), F32)
    dec_spec = pl.BlockSpec((TM_TOK // CHUNK, GLA_KEY_WIDTH), prev)
    wide = jax.ShapeDtypeStruct((t, GLA_WIDTH), BF16)
    return pl.pallas_call(
        _inproj_body,
        grid=(nt + 1,),
        in_specs=[pl.BlockSpec((TM_TOK, D_MODEL), cur), full(w_in_r), full(wg), full(bg),
                  full(trif), full(trib), full(cs)],
        out_specs=[pl.BlockSpec((TM_TOK, GLA_KEY_WIDTH), prev)] * 6
                  + [pl.BlockSpec((TM_TOK, GLA_WIDTH), cur), pl.BlockSpec((TM_TOK, GLA_WIDTH), cur),
                     pl.BlockSpec((TM_TOK, 2 * F_WIDTH), prev), dec_spec, dec_spec],
        out_shape=[kw] * 6 + [wide, wide, jax.ShapeDtypeStruct((t, 2 * F_WIDTH), BF16), dec, dec],
        scratch_shapes=[pltpu.VMEM((TM_TOK, 2 * GLA_KEY_WIDTH), F32)] * 2
                       + [pltpu.VMEM((TM_TOK, F_WIDTH), BF16)] * 2
                       + [pltpu.VMEM((TM_TOK, GATE_COLS), BF16)] * 2
                       + [pltpu.VMEM((2 * GLA_KEY_WIDTH // LANES, TM_TOK, LANES), F32)],
        compiler_params=pltpu.CompilerParams(dimension_semantics=("arbitrary",), vmem_limit_bytes=VMEM_LIMIT),
        name="inproj",
    )(x2, w_in_r, wg, bg, trif, trib, cs)


def _gla_local(q_ref, k_ref, t_ref, v_ref, c, p, causal, m_lo, mv_lo, bd):
    rows = slice(c * CHUNK, (c + 1) * CHUNK)
    kl = slice(p * LANES, (p + 1) * LANES)
    vl = slice(p * 2 * GLA_DV, (p + 1) * 2 * GLA_DV)
    qd = q_ref[rows, kl]
    kd = k_ref[rows, kl]
    kt = t_ref[rows, kl]
    vv = v_ref[rows, vl]
    zk = jnp.zeros_like(kd)
    zv = jnp.zeros_like(vv)
    kbd = jnp.concatenate([jnp.where(m_lo, kd, zk), jnp.where(m_lo, zk, kd)], axis=0)
    att = _dot_nt(qd, kbd)
    att = jnp.where(causal, att, 0.0).astype(BF16)
    vbd = jnp.concatenate([jnp.where(mv_lo, vv, zv), jnp.where(mv_lo, zv, vv)], axis=0)
    kv = jnp.where(bd, _dot_tn(vv, kt), 0.0)
    return qd, att, vbd, kv


def _gla_body(qf_ref, kf_ref, tf_ref, vf_ref, df_ref, qb_ref, kb_ref, tb_ref, vb_ref, db_ref,
              of_ref, ob_ref, sf_ref, sb_ref):
    @pl.when(pl.program_id(1) == 0)
    def _():
        sf_ref[...] = jnp.zeros_like(sf_ref)
        sb_ref[...] = jnp.zeros_like(sb_ref)

    lane = lax.broadcasted_iota(jnp.int32, (CHUNK, LANES), 1)
    row = lax.broadcasted_iota(jnp.int32, (CHUNK, LANES), 0)
    m_lo = lane < GLA_DK
    col = lane & (CHUNK - 1)
    causal_f = row >= col
    causal_b = row <= col
    mv_lo = lax.broadcasted_iota(jnp.int32, (CHUNK, 2 * GLA_DV), 1) < GLA_DV
    bd = ((lax.broadcasted_iota(jnp.int32, (2 * GLA_DV, LANES), 0) < GLA_DV)
          == (lax.broadcasted_iota(jnp.int32, (2 * GLA_DV, LANES), 1) < GLA_DK))
    n = TM_TOK // CHUNK
    pairs = range(GLA_HEADS // 2)
    dirs = ((qf_ref, kf_ref, tf_ref, vf_ref, df_ref, of_ref, sf_ref, causal_f, lambda j: j),
            (qb_ref, kb_ref, tb_ref, vb_ref, db_ref, ob_ref, sb_ref, causal_b, lambda j: n - 1 - j))

    def local(j):
        return [[_gla_local(q, k, t, v, order(j), p, causal, m_lo, mv_lo, bd) for p in pairs]
                for (q, k, t, v, _, _, _, causal, order) in dirs]

    state = [[s_ref[p] for p in pairs] for (_, _, _, _, _, _, s_ref, _, _) in dirs]
    ahead = local(0)
    for j in range(n):
        cur = ahead
        if j + 1 < n:
            ahead = local(j + 1)
        for d, (_, _, _, _, d_ref, o_ref, _, _, order) in enumerate(dirs):
            c = order(j)
            dec = d_ref[c:c + 1, :]
            for p in pairs:
                qd, att, vbd, kv = cur[d][p]
                st = state[d][p]
                o = _dot(att, vbd) + _dot_nt(qd, st.astype(BF16))
                o_ref[c * CHUNK:(c + 1) * CHUNK, p * 2 * GLA_DV:(p + 1) * 2 * GLA_DV] = o.astype(o_ref.dtype)
                state[d][p] = st * dec[:, p * LANES:(p + 1) * LANES] + kv
    for d, (_, _, _, _, _, _, s_ref, _, _) in enumerate(dirs):
        for p in pairs:
            s_ref[p] = state[d][p]


def _gla(qf, kf, tf, qb, kb, tb, v, decf, decb, batch, seq):
    t = batch * seq
    nt = seq // TM_TOK
    fwd = lambda b, i: (b * nt + i, 0)
    bwd = lambda b, i: (b * nt + nt - 1 - i, 0)
    ks = lambda m: pl.BlockSpec((TM_TOK, GLA_KEY_WIDTH), m)
    vs = lambda m: pl.BlockSpec((TM_TOK, GLA_WIDTH), m)
    ds = lambda m: pl.BlockSpec((TM_TOK // CHUNK, GLA_KEY_WIDTH), m)
    o = jax.ShapeDtypeStruct((t, GLA_WIDTH), BF16)
    state = pltpu.VMEM((GLA_HEADS // 2, 2 * GLA_DV, LANES), F32)
    return pl.pallas_call(
        _gla_body,
        grid=(batch, nt),
        in_specs=[ks(fwd), ks(fwd), ks(fwd), vs(fwd), ds(fwd), ks(bwd), ks(bwd), ks(bwd), vs(bwd), ds(bwd)],
        out_specs=[vs(fwd), vs(bwd)],
        out_shape=[o, o],
        scratch_shapes=[state, state],
        compiler_params=pltpu.CompilerParams(dimension_semantics=("arbitrary", "arbitrary"),
                                             vmem_limit_bytes=VMEM_LIMIT),
        name="gla",
    )(qf, kf, tf, v, decf, qb, kb, tb, v, decb)


RADIX_ROWS = 256


def _radix4_body(z_ref, twc_ref, tws_ref, y_ref):
    z = [z_ref[s].astype(F32) for s in range(4)]
    re = [a[:, 0:F_WIDTH] for a in z]
    im = [a[:, F_WIDTH:] for a in z]
    ar, ai = re[0] + re[2], im[0] + im[2]
    br, bi = re[0] - re[2], im[0] - im[2]
    cr, ci = re[1] + re[3], im[1] + im[3]
    dr, di = re[1] - re[3], im[1] - im[3]
    y = [(ar + cr, ai + ci), (br + di, bi - dr), (ar - cr, ai - ci), (br - di, bi + dr)]
    y_ref[0, :, 0:F_WIDTH] = y[0][0].astype(BF16)
    y_ref[0, :, F_WIDTH:] = y[0][1].astype(BF16)
    for k1 in range(1, 4):
        c = jnp.concatenate([twc_ref[k1]] * (F_WIDTH // LANES), axis=1)
        s = jnp.concatenate([tws_ref[k1]] * (F_WIDTH // LANES), axis=1)
        yr, yi = y[k1]
        y_ref[k1, :, 0:F_WIDTH] = (yr * c + yi * s).astype(BF16)
        y_ref[k1, :, F_WIDTH:] = (yi * c - yr * s).astype(BF16)


def _radix4(u, batch):
    z = u.reshape(batch, 4, DFT_N, 2 * F_WIDTH)
    twc, tws = _twiddle_tables(4)
    twc, tws = jnp.asarray(twc), jnp.asarray(tws)
    nr = DFT_N // RADIX_ROWS
    blk = pl.BlockSpec((None, 4, RADIX_ROWS, 2 * F_WIDTH), lambda b, i: (b, 0, i, 0))
    tw = pl.BlockSpec((4, RADIX_ROWS, LANES), lambda b, i: (0, i, 0))
    y = pl.pallas_call(
        _radix4_body,
        grid=(batch, nr),
        in_specs=[blk, tw, tw],
        out_specs=blk,
        out_shape=jax.ShapeDtypeStruct((batch, 4, DFT_N, 2 * F_WIDTH), BF16),
        compiler_params=pltpu.CompilerParams(dimension_semantics=("arbitrary", "arbitrary"),
                                             vmem_limit_bytes=VMEM_LIMIT),
        name="radix4",
    )(z, twc, tws)
    return y


def _seqdft_body(cs_ref, ab_ref, o_ref, *, radix, scale):
    for k1 in range(radix):
        a = ab_ref[k1, :, 0:F_WIDTH]
        b = ab_ref[k1, :, F_WIDTH:]
        res = (_dot(cs_ref[:, 0:DFT_N], a) + _dot(cs_ref[:, DFT_N:], b)) * scale
        for c in range(F_WIDTH // LANES):
            o_ref[c, pl.ds(k1, DFT_ROWS, stride=radix), :] = res[:, c * LANES:(c + 1) * LANES]


def _seqdft(ab, cs, batch, radix):
    scale = float(1.0 / np.sqrt(radix * DFT_N))
    nj = DFT_N // DFT_ROWS
    return pl.pallas_call(
        functools.partial(_seqdft_body, radix=radix, scale=scale),
        grid=(batch, nj),
        in_specs=[pl.BlockSpec((DFT_ROWS, 2 * DFT_N), lambda b, j: (j, 0)),
                  pl.BlockSpec((None, radix, DFT_N, 2 * F_WIDTH), lambda b, j: (b, 0, 0, 0))],
        out_specs=pl.BlockSpec((F_WIDTH // LANES, radix * DFT_ROWS, LANES), lambda b, j: (0, b * nj + j, 0)),
        out_shape=jax.ShapeDtypeStruct((F_WIDTH // LANES, batch * radix * DFT_N, LANES), F32),
        compiler_params=pltpu.CompilerParams(dimension_semantics=("arbitrary", "arbitrary"),
                                             vmem_limit_bytes=VMEM_LIMIT),
        name="seqdft",
    )(cs, ab)


def _first_index(hit, rows):
    return jnp.min(jnp.where(hit, rows.astype(F32), 1e6), axis=0, keepdims=True).astype(jnp.int32)


def _postmix_body(of_ref, ob_ref, r_ref, fo_ref, x_ref, wo_ref, gg_ref, g2_ref, wr_ref,
                  h_ref, xn_ref, eidx_ref, cw_ref, rank_ref, cnt_ref, carry_ref):
    @pl.when(pl.program_id(0) == 0)
    def _():
        carry_ref[...] = jnp.zeros_like(carry_ref)

    o = of_ref[...].astype(F32) + ob_ref[...].astype(F32)
    r = r_ref[...].astype(F32)
    parts = []
    for hd in range(GLA_HEADS):
        sl = slice(hd * GLA_DV, (hd + 1) * GLA_DV)
        oh = o[:, sl]
        oh = oh * lax.rsqrt(jnp.mean(oh * oh, axis=-1, keepdims=True) + EPS)
        rh = r[:, sl]
        parts.append((oh * gg_ref[...] * (rh * jax.nn.sigmoid(rh))).astype(BF16))
    on = jnp.concatenate(parts, axis=1)
    fo = jnp.concatenate([fo_ref[c] for c in range(F_WIDTH // LANES)], axis=1)
    mixed = _dot(on, wo_ref[0:GLA_WIDTH, :]) + _dot(fo.astype(BF16), wo_ref[GLA_WIDTH:, :])
    h = x_ref[...] + mixed
    h_ref[...] = h
    xn = h * lax.rsqrt(jnp.mean(h * h, axis=-1, keepdims=True) + EPS) * g2_ref[...]
    _store_token_tiles(xn_ref, xn)
    logits = _dot_nt(wr_ref[...], xn.astype(BF16))

    sub8 = lax.broadcasted_iota(jnp.int32, (8, TM_TOK), 0)
    lg = jnp.where(sub8 < N_GROUPS, logits[0:8], -jnp.inf)
    gmax = jnp.max(lg, axis=0, keepdims=True)
    g_w = 1.0 / jnp.sum(jnp.exp(lg - gmax), axis=0, keepdims=True)
    g_sel = _first_index(lg == gmax, sub8)
    sel = logits[8:16]
    for g in range(1, N_GROUPS):
        sel = jnp.where(g_sel == g, logits[8 + 8 * g:16 + 8 * g], sel)
    m1 = jnp.max(sel, axis=0, keepdims=True)
    i1 = _first_index(sel == m1, sub8)
    sel2 = jnp.where(sub8 == i1, -jnp.inf, sel)
    m2 = jnp.max(sel2, axis=0, keepdims=True)
    i2 = _first_index(sel2 == m2, sub8)
    e21 = jnp.exp(m2 - m1)
    w1 = 1.0 / (1.0 + e21)
    w2 = e21 / (1.0 + e21)
    e1 = g_sel * EXPERTS_PER_GROUP + i1
    e2 = g_sel * EXPERTS_PER_GROUP + i2
    eidx_ref[...] = jnp.concatenate([e1, e2], axis=0)
    cw_ref[...] = jnp.concatenate([g_w * w1, g_w * w2], axis=0)

    sub = lax.broadcasted_iota(jnp.int32, (N_EXPERTS, TM_TOK), 0)
    oh1 = sub == e1
    oh2 = sub == e2
    oh1b = jnp.where(oh1, 1.0, 0.0).astype(BF16)
    oh2b = jnp.where(oh2, 1.0, 0.0).astype(BF16)
    before = (lax.broadcasted_iota(jnp.int32, (TM_TOK, TM_TOK), 0)
              < lax.broadcasted_iota(jnp.int32, (TM_TOK, TM_TOK), 1))
    before = jnp.where(before, 1.0, 0.0).astype(BF16)
    ones = jnp.ones((TM_TOK, LANES), BF16)
    p1 = _dot(oh1b, before)
    p2 = _dot(oh2b, before)
    c1 = _dot(oh1b, ones)
    c2 = _dot(oh2b, ones)
    carry = carry_ref[...]
    rep = TM_TOK // LANES
    base1 = jnp.concatenate([carry] * rep, axis=1)
    base2 = jnp.concatenate([carry + c1] * rep, axis=1)
    rk1 = jnp.sum(jnp.where(oh1, p1 + base1, 0.0), axis=0, keepdims=True)
    rk2 = jnp.sum(jnp.where(oh2, p2 + base2, 0.0), axis=0, keepdims=True)
    rank_ref[...] = jnp.concatenate([rk1, rk2], axis=0).astype(jnp.int32)
    carry = carry + c1 + c2
    carry_ref[...] = carry
    cnt_ref[...] = carry


def _postmix(of, ob, r, fo, x2, wo, gg, g2, wr):
    t = x2.shape[0]
    nt = t // TM_TOK
    tok = lambda w: pl.BlockSpec((TM_TOK, w), lambda i: (i, 0))
    full = lambda a: pl.BlockSpec(a.shape, lambda i: (0,) * a.ndim)
    lane2 = pl.BlockSpec((2, TM_TOK), lambda i: (0, i))
    return pl.pallas_call(
        _postmix_body,
        grid=(nt,),
        in_specs=[tok(GLA_WIDTH), tok(GLA_WIDTH), tok(GLA_WIDTH),
                  pl.BlockSpec((F_WIDTH // LANES, TM_TOK, LANES), lambda i: (0, i, 0)), tok(D_MODEL),
                  full(wo), full(gg), full(g2), full(wr)],
        out_specs=[tok(D_MODEL), pl.BlockSpec((TM_TOK * TOK_ROWS, LANES), lambda i: (i, 0)), lane2, lane2, lane2,
                   pl.BlockSpec((N_EXPERTS, LANES), lambda i: (0, 0))],
        out_shape=[jax.ShapeDtypeStruct((t, D_MODEL), F32), jax.ShapeDtypeStruct((t * TOK_ROWS, LANES), F32),
                   jax.ShapeDtypeStruct((2, t), jnp.int32), jax.ShapeDtypeStruct((2, t), F32),
                   jax.ShapeDtypeStruct((2, t), jnp.int32), jax.ShapeDtypeStruct((N_EXPERTS, LANES), F32)],
        scratch_shapes=[pltpu.VMEM((N_EXPERTS, LANES), F32)],
        compiler_params=pltpu.CompilerParams(dimension_semantics=("arbitrary",), vmem_limit_bytes=VMEM_LIMIT),
        name="postmix",
    )(of, ob, r, fo, x2, wo, gg, g2, wr)


def _store_token_tiles(ref, val):
    n = val.shape[0]
    for c in range(TOK_ROWS):
        ref[pl.ds(c, n, stride=TOK_ROWS), :] = val[:, c * LANES:(c + 1) * LANES]


def _load_token_tiles(ref, n):
    return jnp.concatenate([ref[pl.ds(c, n, stride=TOK_ROWS), :] for c in range(TOK_ROWS)], axis=1)


def _token_rows(tok):
    if isinstance(tok, int):
        return pl.ds(tok * TOK_ROWS, TOK_ROWS)
    return pl.ds(pl.multiple_of(tok * TOK_ROWS, TOK_ROWS), TOK_ROWS)


def _row_copy(src_hbm, dst_hbm, src_tok, dst_tok, sem):
    return pltpu.make_async_copy(src_hbm.at[_token_rows(src_tok)], dst_hbm.at[_token_rows(dst_tok)], sem)


def _dispatch_body(fill_ref, pos_ref, xn_ref, xs_hbm, zeros_ref, fill_sem, row_sem):
    i = pl.program_id(0)

    @pl.when(i == 0)
    def _():
        zeros_ref[...] = jnp.zeros_like(zeros_ref)
        tile_rows = TM_EXP * TOK_ROWS

        def fill(e):
            rows = pl.ds(pl.multiple_of(fill_ref[e] * tile_rows, tile_rows), tile_rows)
            return pltpu.make_async_copy(zeros_ref, xs_hbm.at[rows], fill_sem)

        for e in range(N_FILL):
            @pl.when(fill_ref[e] >= 0)
            def _():
                fill(e).start()
        for e in range(N_FILL):
            @pl.when(fill_ref[e] >= 0)
            def _():
                fill(e).wait()

    for j in range(TM_DISP):
        for s in range(2):
            _row_copy(xn_ref, xs_hbm, j, pos_ref[s, j], row_sem).start(priority=s)
    for j in range(TM_DISP):
        for s in range(2):
            _row_copy(xn_ref, xs_hbm, j, 0, row_sem).wait()


def _dispatch(xn, pos, fill_rows, rows_total):
    t = xn.shape[0] // TOK_ROWS
    grid_spec = pltpu.PrefetchScalarGridSpec(
        num_scalar_prefetch=1,
        grid=(t // TM_DISP,),
        in_specs=[pl.BlockSpec((2, TM_DISP), lambda i, fill: (0, i), memory_space=pltpu.SMEM),
                  pl.BlockSpec((TM_DISP * TOK_ROWS, LANES), lambda i, fill: (i, 0))],
        out_specs=pl.BlockSpec(memory_space=pl.ANY),
        scratch_shapes=[pltpu.VMEM((TM_EXP * TOK_ROWS, LANES), F32), pltpu.SemaphoreType.DMA(()),
                        pltpu.SemaphoreType.DMA(())],
    )
    return pl.pallas_call(
        _dispatch_body,
        grid_spec=grid_spec,
        out_shape=jax.ShapeDtypeStruct((rows_total * TOK_ROWS, LANES), F32),
        compiler_params=pltpu.CompilerParams(dimension_semantics=("arbitrary",), vmem_limit_bytes=VMEM_LIMIT),
        name="dispatch",
    )(fill_rows, pos, xn)


def _experts_body(te_ref, na_ref, xs_ref, wg_ref, wu_ref, wd_ref, ys_ref):
    @pl.when(pl.program_id(0) < na_ref[0])
    def _():
        x = _load_token_tiles(xs_ref, TM_EXP).astype(BF16)
        gate = _dot(x, wg_ref[...])
        up = _dot(x, wu_ref[...])
        hid = (gate * jax.nn.sigmoid(gate) * up).astype(BF16)
        _store_token_tiles(ys_ref, _dot(hid, wd_ref[...]))

    @pl.when(pl.program_id(0) >= na_ref[0])
    def _():
        ys_ref[...] = jnp.zeros_like(ys_ref)


def _experts(xs, tile_expert, n_active, wg, wu, wd):
    n_tiles = xs.shape[0] // (TM_EXP * TOK_ROWS)
    row = lambda i, te, na: (i, 0)
    wsel = lambda i, te, na: (te[jnp.minimum(i, na[0] - 1)], 0, 0)
    grid_spec = pltpu.PrefetchScalarGridSpec(
        num_scalar_prefetch=2,
        grid=(n_tiles,),
        in_specs=[pl.BlockSpec((TM_EXP * TOK_ROWS, LANES), row),
                  pl.BlockSpec((None, D_MODEL, D_EXPERT), wsel),
                  pl.BlockSpec((None, D_MODEL, D_EXPERT), wsel),
                  pl.BlockSpec((None, D_EXPERT, D_MODEL), wsel)],
        out_specs=pl.BlockSpec((TM_EXP * TOK_ROWS, LANES), row),
    )
    return pl.pallas_call(
        _experts_body,
        grid_spec=grid_spec,
        out_shape=jax.ShapeDtypeStruct(xs.shape, F32),
        compiler_params=pltpu.CompilerParams(dimension_semantics=("arbitrary",), vmem_limit_bytes=VMEM_LIMIT),
        name="experts",
    )(tile_expert, n_active, xs, wg, wu, wd)


def _combine_body(pos_ref, nxt_ref, ys_hbm, cw_ref, h_ref, gf_ref, y_ref, buf_ref, sem):
    i = pl.program_id(0)
    slot = lax.rem(i, 2)

    def row_gather(p_ref, sl, s, j):
        return pltpu.make_async_copy(ys_hbm.at[_token_rows(p_ref[s, j])], buf_ref.at[sl, s, _token_rows(j)],
                                     sem.at[sl])

    def issue_tile(p_ref, sl):
        for j in range(TM_COMB):
            for s in range(2):
                row_gather(p_ref, sl, s, j).start(priority=s)

    @pl.when(i == 0)
    def _():
        issue_tile(pos_ref, slot)

    @pl.when(i + 1 < pl.num_programs(0))
    def _():
        issue_tile(nxt_ref, 1 - slot)

    for j in range(TM_COMB):
        for s in range(2):
            pltpu.make_async_copy(ys_hbm.at[_token_rows(0)], buf_ref.at[slot, s, _token_rows(j)], sem.at[slot]).wait()

    cw = cw_ref[...]
    y0 = _load_token_tiles(buf_ref.at[slot, 0], TM_COMB)
    y1 = _load_token_tiles(buf_ref.at[slot, 1], TM_COMB)
    h = h_ref[...] + cw[:, 0:1] * y0 + cw[:, 1:2] * y1
    y_ref[...] = h * lax.rsqrt(jnp.mean(h * h, axis=-1, keepdims=True) + EPS) * gf_ref[...]


def _combine(ys, pos, cw_t, h, gf):
    t = h.shape[0]
    n = t // TM_COMB
    tok = pl.BlockSpec((TM_COMB, D_MODEL), lambda i: (i, 0))
    return pl.pallas_call(
        _combine_body,
        grid=(n,),
        in_specs=[pl.BlockSpec((2, TM_COMB), lambda i: (0, i), memory_space=pltpu.SMEM),
                  pl.BlockSpec((2, TM_COMB), lambda i: (0, jnp.minimum(i + 1, n - 1)), memory_space=pltpu.SMEM),
                  pl.BlockSpec(memory_space=pl.ANY),
                  pl.BlockSpec((TM_COMB, 2), lambda i: (i, 0)),
                  tok,
                  pl.BlockSpec((1, D_MODEL), lambda i: (0, 0))],
        out_specs=tok,
        out_shape=jax.ShapeDtypeStruct((t, D_MODEL), F32),
        scratch_shapes=[pltpu.VMEM((2, 2, TM_COMB * TOK_ROWS, LANES), F32), pltpu.SemaphoreType.DMA((2,))],
        compiler_params=pltpu.CompilerParams(dimension_semantics=("arbitrary",), vmem_limit_bytes=VMEM_LIMIT),
        name="combine",
    )(pos, pos, ys, cw_t, h, gf)


def _prepare_params(norm1_g, w_in, w_gk2_f, b_gk_f, w_gk2_b, b_gk_b, gla_norm_g, w_out,
                    norm2_g, w_group, w_expert, w_gate, w_up, w_down, norm_f_g):
    w = w_in[0] * norm1_g[0][:, None]
    gate_lo = 2 * GLA_KEY_WIDTH + 2 * GLA_WIDTH
    gate_hi = gate_lo + 2 * GATE_RANK
    w_in_r = jnp.concatenate([w[:, :gate_lo], w[:, gate_hi:], w[:, gate_lo:gate_hi],
                              jnp.zeros((D_MODEL, GATE_COLS - 2 * GATE_RANK), F32)], axis=1).astype(BF16)
    zk = jnp.zeros((GATE_RANK, GLA_KEY_WIDTH), F32)
    wg = jnp.concatenate([jnp.concatenate([w_gk2_f[0], zk], axis=1), jnp.concatenate([zk, w_gk2_b[0]], axis=1),
                          jnp.zeros((GATE_COLS - 2 * GATE_RANK, 2 * GLA_KEY_WIDTH), F32)], axis=0).astype(BF16)
    bg = jnp.concatenate([b_gk_f[0], b_gk_b[0]])[None, :]
    wr = jnp.concatenate([w_group[0].T, jnp.zeros((8 - N_GROUPS, D_MODEL), F32), w_expert[0].T,
                          jnp.zeros((ROUTER_ROWS - 8 - N_EXPERTS, D_MODEL), F32)], axis=0).astype(BF16)
    return dict(
        w_in_r=w_in_r, wg=wg, bg=bg,
        gg=gla_norm_g[0][None, :], wo=w_out[0].astype(BF16), g2=norm2_g[0][None, :], wr=wr,
        w_gate=w_gate[0].astype(BF16), w_up=w_up[0].astype(BF16), w_down=w_down[0].astype(BF16),
        gf=norm_f_g[None, :],
        trif=jnp.asarray(_TRI_F).astype(BF16), trib=jnp.asarray(_TRI_B).astype(BF16),
        chan_dft=jnp.asarray(_CHAN_DFT).astype(BF16), seq_dft=jnp.asarray(_SEQ_DFT).astype(BF16))


def _encoder(x, p):
    batch, seq, _ = x.shape
    t = batch * seq
    assert seq % DFT_N == 0 and seq // DFT_N in (1, 4), "sequence DFT supports seq = 2048 or 8192"
    radix = seq // DFT_N
    x2 = x.reshape(t, D_MODEL)
    qf, kf, tf, qb, kb, tb, v, r, u, decf, decb = _inproj(x2, p["w_in_r"], p["wg"], p["bg"],
                                                          p["trif"], p["trib"], p["chan_dft"])
    of, ob = _gla(qf, kf, tf, qb, kb, tb, v, decf, decb, batch, seq)
    if radix == 1:
        ab = u.reshape(batch, 1, DFT_N, 2 * F_WIDTH)
    else:
        ab = _radix4(u, batch)
    fo = _seqdft(ab, p["seq_dft"], batch, radix)
    h, xn, eidx, cw, rank, cnt = _postmix(of, ob, r, fo, x2, p["wo"], p["gg"], p["g2"], p["wr"])

    counts = cnt[:, 0].astype(jnp.int32)
    tiles = (counts + TM_EXP - 1) // TM_EXP
    tile_end = jnp.cumsum(tiles)
    tile_start = tile_end - tiles
    experts = jnp.arange(N_EXPERTS, dtype=jnp.int32)
    seg_row = jnp.sum(jnp.where(eidx[:, :, None] == experts, tile_start * TM_EXP, 0), axis=-1)
    pos = seg_row + rank
    n_tiles = 2 * t // TM_EXP + N_EXPERTS
    tile_ids = jnp.arange(n_tiles, dtype=jnp.int32)
    tile_expert = jnp.minimum(jnp.sum((tile_end[None, :] <= tile_ids[:, None]).astype(jnp.int32), axis=1),
                              N_EXPERTS - 1)
    n_active = tile_end[-1:].astype(jnp.int32)
    tail = n_active + jnp.arange(N_EXPERTS, dtype=jnp.int32)
    fill_tiles = jnp.concatenate([jnp.where(tiles > 0, tile_end - 1, -1),
                                  jnp.where(tail < n_tiles, tail, -1)]).astype(jnp.int32)

    xs = _dispatch(xn, pos, fill_tiles, n_tiles * TM_EXP)
    ys = _experts(xs, tile_expert, n_active, p["w_gate"], p["w_up"], p["w_down"])
    y = _combine(ys, pos, cw.T, h, p["gf"])
    return y.reshape(batch, seq, D_MODEL)


def kernel(x_prompt, x_sample, norm1_g, w_in, w_gk2_f, b_gk_f, w_gk2_b, b_gk_b, gla_norm_g, w_out, norm2_g,
           w_group, w_expert, w_gate, w_up, w_down, norm_f_g):
    p = _prepare_params(norm1_g, w_in, w_gk2_f, b_gk_f, w_gk2_b, b_gk_b, gla_norm_g, w_out,
                        norm2_g, w_group, w_expert, w_gate, w_up, w_down, norm_f_g)
    return (_encoder(x_prompt, p), _encoder(x_sample, p))
```

```python
import functools

import numpy as np
import jax
import jax.numpy as jnp
from jax import lax
from jax.experimental import pallas as pl
from jax.experimental.pallas import tpu as pltpu

D_MODEL = 1024
EPS = 1e-6
GLA_HEADS = 4
GLA_DV = 128
GLA_DK = 64
GLA_WIDTH = GLA_HEADS * GLA_DV
GLA_KEY_WIDTH = GLA_HEADS * GLA_DK
GATE_RANK = 16
GATE_NORMALIZER = 16.0
CHUNK = 64
F_GROUPS = 4
F_GROUP_DIM = 128
F_WIDTH = F_GROUPS * F_GROUP_DIM
N_GROUPS = 4
EXPERTS_PER_GROUP = 8
N_EXPERTS = N_GROUPS * EXPERTS_PER_GROUP
D_EXPERT = 256

LANES = 128
V7X_VMEM_BYTES = 64 * 1024 * 1024
VMEM_LIMIT = 56 * 1024 * 1024

TM_TOK = 512
SUB = 128
DFT_N = 2048
DFT_ROWS = 512
TM_EXP = 512
TM_DISP = 512
TM_COMB = 256
ROUTER_ROWS = 48
GATE_COLS = 128
IN_COLS_PAD = 2048 + GATE_COLS
TOK_ROWS = D_MODEL // LANES
N_FILL = 2 * N_EXPERTS

BF16 = jnp.bfloat16
F32 = jnp.float32


def _dot(a, b):
    return jnp.dot(a, b, preferred_element_type=F32)


def _interleave(*stages):
    live = list(stages)
    while live:
        for g in list(live):
            try:
                next(g)
            except StopIteration:
                live.remove(g)


def _dot_nt(a, b):
    return lax.dot_general(a, b, (((1,), (1,)), ((), ())), preferred_element_type=F32)


def _dot_tn(a, b):
    return lax.dot_general(a, b, (((0,), (0,)), ((), ())), preferred_element_type=F32)


def _tri_tables():
    r = np.arange(SUB)
    same = (r[:, None] // CHUNK) == (r[None, :] // CHUNK)
    l_incl = same & (r[None, :] <= r[:, None])
    u_strict = same & (r[None, :] > r[:, None])
    u_incl = same & (r[None, :] >= r[:, None])
    l_strict = same & (r[None, :] < r[:, None])
    fwd = np.concatenate([l_incl, u_strict], 0).astype(np.float32)
    bwd = np.concatenate([u_incl, l_strict], 0).astype(np.float32)
    return fwd, bwd


def _chan_dft_table():
    c = np.arange(F_GROUP_DIM)
    ang = 2.0 * np.pi * ((c[:, None] * c[None, :]) % F_GROUP_DIM) / F_GROUP_DIM
    s = 1.0 / np.sqrt(F_GROUP_DIM)
    return np.concatenate([np.cos(ang) * s, -np.sin(ang) * s], 1).astype(np.float32)


def _seq_dft_tables():
    k = np.arange(DFT_N // 2, dtype=np.int64)
    s = np.arange(DFT_N, dtype=np.int64)
    ang = 2.0 * np.pi * ((k[:, None] * s[None, :]) % DFT_N) / DFT_N
    lower = np.concatenate([np.cos(ang), np.sin(ang)], 1).astype(np.float32)
    edge = np.zeros((16, 2 * DFT_N), np.float32)
    edge[0, :DFT_N] = 1.0 - 2.0 * (s % 2)
    edge[1, :DFT_N] = np.array([1.0, 0.0, -1.0, 0.0])[s % 4]
    edge[1, DFT_N:] = -np.array([0.0, 1.0, 0.0, -1.0])[s % 4]
    i = np.arange(DFT_ROWS)
    mirror = ((i[:, None] >= 1) & (i[None, :] == DFT_ROWS - i[:, None])).astype(np.float32)
    return lower, edge, mirror


def _twiddle_tables(radix):
    k1 = np.arange(radix, dtype=np.int64)[:, None]
    s2 = np.arange(DFT_N, dtype=np.int64)[None, :]
    ang = 2.0 * np.pi * ((k1 * s2) % (radix * DFT_N)) / (radix * DFT_N)
    c = np.repeat(np.cos(ang)[:, :, None], LANES, 2).astype(np.float32)
    s = np.repeat(np.sin(ang)[:, :, None], LANES, 2).astype(np.float32)
    return c, s


_TRI_F, _TRI_B = _tri_tables()
_CHAN_DFT = _chan_dft_table()
_SEQ_DFT_LOWER, _SEQ_DFT_EDGE, _SEQ_DFT_MIRROR = _seq_dft_tables()


def _inproj_project(x_ref, w_ref, v_ref, r_ref, qk_s, fx_s, gt_s):
    x = x_ref[...]
    inv = lax.rsqrt(jnp.mean(x * x, axis=-1, keepdims=True) + EPS)
    xb = x.astype(BF16)
    yield
    qk_s[...] = _dot(xb, w_ref[:, 0:2 * GLA_KEY_WIDTH]) * inv
    yield
    v_ref[...] = (_dot(xb, w_ref[:, 512:1024]) * inv).astype(BF16)
    yield
    r_ref[...] = (_dot(xb, w_ref[:, 1024:1536]) * inv).astype(BF16)
    yield
    fx_s[...] = (_dot(xb, w_ref[:, 1536:2048]) * inv).astype(BF16)
    yield
    gt_s[...] = (_dot(xb, w_ref[:, 2048:IN_COLS_PAD]) * inv).astype(BF16)


def _inproj_decay(qk_s, fx_s, gt_s, wg_ref, bg_ref, trif_ref, trib_ref, cs_ref,
                  qf_ref, kf_ref, tf_ref, qb_ref, kb_ref, tb_ref, u_ref, decf_ref, decb_ref, tot_ref):
    z = _dot(gt_s[...], wg_ref[...]) + bg_ref[...]
    la = (jnp.minimum(z, 0.0) - jnp.log1p(jnp.exp(-jnp.abs(z)))) * (1.0 / GATE_NORMALIZER)
    la_hi = la.astype(BF16)
    la_lo = (la - la_hi.astype(F32)).astype(BF16)
    trif = trif_ref[...]
    trib = trib_ref[...]
    scale = GLA_DK ** -0.5
    for s in range(TM_TOK // SUB):
        yield
        rows = slice(s * SUB, (s + 1) * SUB)
        q = qk_s[rows, 0:GLA_KEY_WIDTH]
        k = qk_s[rows, GLA_KEY_WIDTH:2 * GLA_KEY_WIDTH]
        rf = _dot(trif, la_hi[rows, 0:GLA_KEY_WIDTH]) + _dot(trif, la_lo[rows, 0:GLA_KEY_WIDTH])
        b, tl = rf[0:SUB], rf[SUB:2 * SUB]
        qf_ref[rows, :] = (q * scale * jnp.exp(b)).astype(BF16)
        kf_ref[rows, :] = (k * jnp.exp(-b)).astype(BF16)
        tf_ref[rows, :] = (k * jnp.exp(tl)).astype(BF16)
        totf = b + tl
        tot_ref[0, rows, :] = totf[:, 0:LANES]
        tot_ref[1, rows, :] = totf[:, LANES:]
        rb = _dot(trib, la_hi[rows, GLA_KEY_WIDTH:]) + _dot(trib, la_lo[rows, GLA_KEY_WIDTH:])
        c, tlb = rb[0:SUB], rb[SUB:2 * SUB]
        qb_ref[rows, :] = (q * scale * jnp.exp(c)).astype(BF16)
        kb_ref[rows, :] = (k * jnp.exp(-c)).astype(BF16)
        tb_ref[rows, :] = (k * jnp.exp(tlb)).astype(BF16)
        totb = c + tlb
        tot_ref[2, rows, :] = totb[:, 0:LANES]
        tot_ref[3, rows, :] = totb[:, LANES:]
    yield
    chunk_rows = pl.ds(0, TM_TOK // CHUNK, stride=CHUNK)
    decf_ref[:, 0:LANES] = jnp.exp(tot_ref[0, chunk_rows, :])
    decf_ref[:, LANES:] = jnp.exp(tot_ref[1, chunk_rows, :])
    decb_ref[:, 0:LANES] = jnp.exp(tot_ref[2, chunk_rows, :])
    decb_ref[:, LANES:] = jnp.exp(tot_ref[3, chunk_rows, :])
    cs = cs_ref[...]
    for g in range(F_GROUPS):
        res = _dot(fx_s[:, g * LANES:(g + 1) * LANES], cs)
        u_ref[:, g * LANES:(g + 1) * LANES] = res[:, 0:LANES].astype(BF16)
        u_ref[:, F_WIDTH + g * LANES:F_WIDTH + (g + 1) * LANES] = res[:, LANES:].astype(BF16)


def _inproj_body(x_ref, w_ref, wg_ref, bg_ref, trif_ref, trib_ref, cs_ref,
                 qf_ref, kf_ref, tf_ref, qb_ref, kb_ref, tb_ref, v_ref, r_ref, u_ref, decf_ref, decb_ref,
                 qk0, qk1, fx0, fx1, gt0, gt1, tot_ref):
    i = pl.program_id(0)

    @pl.when(i == 0)
    def _():
        qk1[...] = jnp.zeros_like(qk1)
        fx1[...] = jnp.zeros_like(fx1)
        gt1[...] = jnp.zeros_like(gt1)

    def step(cur, prev):
        stage_a = _inproj_project(x_ref, w_ref, v_ref, r_ref, *cur)
        stage_b = _inproj_decay(*prev, wg_ref, bg_ref, trif_ref, trib_ref, cs_ref,
                                qf_ref, kf_ref, tf_ref, qb_ref, kb_ref, tb_ref, u_ref, decf_ref, decb_ref, tot_ref)
        _interleave(stage_a, stage_b)

    @pl.when(lax.rem(i, 2) == 0)
    def _():
        step((qk0, fx0, gt0), (qk1, fx1, gt1))

    @pl.when(lax.rem(i, 2) == 1)
    def _():
        step((qk1, fx1, gt1), (qk0, fx0, gt0))


def _inproj(x2, w_in_r, wg, bg, trif, trib, cs):
    t = x2.shape[0]
    nt = t // TM_TOK
    cur = lambda i: (jnp.minimum(i, nt - 1), 0)
    prev = lambda i: (jnp.maximum(i - 1, 0), 0)
    full = lambda a: pl.BlockSpec(a.shape, lambda i: (0,) * a.ndim)
    kw = jax.ShapeDtypeStruct((t, GLA_KEY_WIDTH), BF16)
    dec = jax.ShapeDtypeStruct((t // CHUNK, GLA_KEY_WIDTH), F32)
    dec_spec = pl.BlockSpec((TM_TOK // CHUNK, GLA_KEY_WIDTH), prev)
    wide = jax.ShapeDtypeStruct((t, GLA_WIDTH), BF16)
    return pl.pallas_call(
        _inproj_body,
        grid=(nt + 1,),
        in_specs=[pl.BlockSpec((TM_TOK, D_MODEL), cur), full(w_in_r), full(wg), full(bg),
                  full(trif), full(trib), full(cs)],
        out_specs=[pl.BlockSpec((TM_TOK, GLA_KEY_WIDTH), prev)] * 6
                  + [pl.BlockSpec((TM_TOK, GLA_WIDTH), cur), pl.BlockSpec((TM_TOK, GLA_WIDTH), cur),
                     pl.BlockSpec((TM_TOK, 2 * F_WIDTH), prev), dec_spec, dec_spec],
        out_shape=[kw] * 6 + [wide, wide, jax.ShapeDtypeStruct((t, 2 * F_WIDTH), BF16), dec, dec],
        scratch_shapes=[pltpu.VMEM((TM_TOK, 2 * GLA_KEY_WIDTH), F32)] * 2
                       + [pltpu.VMEM((TM_TOK, F_WIDTH), BF16)] * 2
                       + [pltpu.VMEM((TM_TOK, GATE_COLS), BF16)] * 2
                       + [pltpu.VMEM((2 * GLA_KEY_WIDTH // LANES, TM_TOK, LANES), F32)],
        compiler_params=pltpu.CompilerParams(dimension_semantics=("arbitrary",), vmem_limit_bytes=VMEM_LIMIT),
        name="inproj",
    )(x2, w_in_r, wg, bg, trif, trib, cs)


def _gla_local(q_ref, k_ref, t_ref, v_ref, c, p, causal, m_lo, mv_lo, bd):
    rows = slice(c * CHUNK, (c + 1) * CHUNK)
    kl = slice(p * LANES, (p + 1) * LANES)
    vl = slice(p * 2 * GLA_DV, (p + 1) * 2 * GLA_DV)
    qd = q_ref[rows, kl]
    kd = k_ref[rows, kl]
    kt = t_ref[rows, kl]
    vv = v_ref[rows, vl]
    zk = jnp.zeros_like(kd)
    zv = jnp.zeros_like(vv)
    kbd = jnp.concatenate([jnp.where(m_lo, kd, zk), jnp.where(m_lo, zk, kd)], axis=0)
    att = _dot_nt(qd, kbd)
    att = jnp.where(causal, att, 0.0).astype(BF16)
    vbd = jnp.concatenate([jnp.where(mv_lo, vv, zv), jnp.where(mv_lo, zv, vv)], axis=0)
    kv = jnp.where(bd, _dot_tn(vv, kt), 0.0)
    return qd, att, vbd, kv


def _gla_body(qf_ref, kf_ref, tf_ref, vf_ref, df_ref, qb_ref, kb_ref, tb_ref, vb_ref, db_ref,
              of_ref, ob_ref, sf_ref, sb_ref):
    @pl.when(pl.program_id(1) == 0)
    def _():
        sf_ref[...] = jnp.zeros_like(sf_ref)
        sb_ref[...] = jnp.zeros_like(sb_ref)

    lane = lax.broadcasted_iota(jnp.int32, (CHUNK, LANES), 1)
    row = lax.broadcasted_iota(jnp.int32, (CHUNK, LANES), 0)
    m_lo = lane < GLA_DK
    col = lane & (CHUNK - 1)
    causal_f = row >= col
    causal_b = row <= col
    mv_lo = lax.broadcasted_iota(jnp.int32, (CHUNK, 2 * GLA_DV), 1) < GLA_DV
    bd = ((lax.broadcasted_iota(jnp.int32, (2 * GLA_DV, LANES), 0) < GLA_DV)
          == (lax.broadcasted_iota(jnp.int32, (2 * GLA_DV, LANES), 1) < GLA_DK))
    n = TM_TOK // CHUNK
    pairs = range(GLA_HEADS // 2)
    dirs = ((qf_ref, kf_ref, tf_ref, vf_ref, df_ref, of_ref, sf_ref, causal_f, lambda j: j),
            (qb_ref, kb_ref, tb_ref, vb_ref, db_ref, ob_ref, sb_ref, causal_b, lambda j: n - 1 - j))

    def local(j):
        return [[_gla_local(q, k, t, v, order(j), p, causal, m_lo, mv_lo, bd) for p in pairs]
                for (q, k, t, v, _, _, _, causal, order) in dirs]

    state = [[s_ref[p] for p in pairs] for (_, _, _, _, _, _, s_ref, _, _) in dirs]
    ahead = local(0)
    for j in range(n):
        cur = ahead
        if j + 1 < n:
            ahead = local(j + 1)
        for d, (_, _, _, _, d_ref, o_ref, _, _, order) in enumerate(dirs):
            c = order(j)
            dec = d_ref[c:c + 1, :]
            for p in pairs:
                qd, att, vbd, kv = cur[d][p]
                st = state[d][p]
                o = _dot(att, vbd) + _dot_nt(qd, st.astype(BF16))
                o_ref[c * CHUNK:(c + 1) * CHUNK, p * 2 * GLA_DV:(p + 1) * 2 * GLA_DV] = o.astype(o_ref.dtype)
                state[d][p] = st * dec[:, p * LANES:(p + 1) * LANES] + kv
    for d, (_, _, _, _, _, _, s_ref, _, _) in enumerate(dirs):
        for p in pairs:
            s_ref[p] = state[d][p]


def _gla(qf, kf, tf, qb, kb, tb, v, decf, decb, batch, seq):
    t = batch * seq
    nt = seq // TM_TOK
    fwd = lambda b, i: (b * nt + i, 0)
    bwd = lambda b, i: (b * nt + nt - 1 - i, 0)
    ks = lambda m: pl.BlockSpec((TM_TOK, GLA_KEY_WIDTH), m)
    vs = lambda m: pl.BlockSpec((TM_TOK, GLA_WIDTH), m)
    ds = lambda m: pl.BlockSpec((TM_TOK // CHUNK, GLA_KEY_WIDTH), m)
    o = jax.ShapeDtypeStruct((t, GLA_WIDTH), BF16)
    state = pltpu.VMEM((GLA_HEADS // 2, 2 * GLA_DV, LANES), F32)
    return pl.pallas_call(
        _gla_body,
        grid=(batch, nt),
        in_specs=[ks(fwd), ks(fwd), ks(fwd), vs(fwd), ds(fwd), ks(bwd), ks(bwd), ks(bwd), vs(bwd), ds(bwd)],
        out_specs=[vs(fwd), vs(bwd)],
        out_shape=[o, o],
        scratch_shapes=[state, state],
        compiler_params=pltpu.CompilerParams(dimension_semantics=("arbitrary", "arbitrary"),
                                             vmem_limit_bytes=VMEM_LIMIT),
        name="gla",
    )(qf, kf, tf, v, decf, qb, kb, tb, v, decb)


RADIX_ROWS = 256


def _radix4_body(z_ref, twc_ref, tws_ref, y_ref):
    z = [z_ref[s].astype(F32) for s in range(4)]
    re = [a[:, 0:F_WIDTH] for a in z]
    im = [a[:, F_WIDTH:] for a in z]
    ar, ai = re[0] + re[2], im[0] + im[2]
    br, bi = re[0] - re[2], im[0] - im[2]
    cr, ci = re[1] + re[3], im[1] + im[3]
    dr, di = re[1] - re[3], im[1] - im[3]
    y = [(ar + cr, ai + ci), (br + di, bi - dr), (ar - cr, ai - ci), (br - di, bi + dr)]
    y_ref[0, :, 0:F_WIDTH] = y[0][0].astype(BF16)
    y_ref[0, :, F_WIDTH:] = y[0][1].astype(BF16)
    for k1 in range(1, 4):
        c = jnp.concatenate([twc_ref[k1]] * (F_WIDTH // LANES), axis=1)
        s = jnp.concatenate([tws_ref[k1]] * (F_WIDTH // LANES), axis=1)
        yr, yi = y[k1]
        y_ref[k1, :, 0:F_WIDTH] = (yr * c + yi * s).astype(BF16)
        y_ref[k1, :, F_WIDTH:] = (yi * c - yr * s).astype(BF16)


def _radix4(u, batch):
    z = u.reshape(batch, 4, DFT_N, 2 * F_WIDTH)
    twc, tws = _twiddle_tables(4)
    twc, tws = jnp.asarray(twc), jnp.asarray(tws)
    nr = DFT_N // RADIX_ROWS
    blk = pl.BlockSpec((None, 4, RADIX_ROWS, 2 * F_WIDTH), lambda b, i: (b, 0, i, 0))
    tw = pl.BlockSpec((4, RADIX_ROWS, LANES), lambda b, i: (0, i, 0))
    y = pl.pallas_call(
        _radix4_body,
        grid=(batch, nr),
        in_specs=[blk, tw, tw],
        out_specs=blk,
        out_shape=jax.ShapeDtypeStruct((batch, 4, DFT_N, 2 * F_WIDTH), BF16),
        compiler_params=pltpu.CompilerParams(dimension_semantics=("arbitrary", "arbitrary"),
                                             vmem_limit_bytes=VMEM_LIMIT),
        name="radix4",
    )(z, twc, tws)
    return y


def _seqdft_body(cs_ref, edge_ref, mir_ref, ab_ref, o_ref, *, radix, scale):
    j = pl.program_id(1)
    first_row = lax.broadcasted_iota(jnp.int32, (DFT_ROWS, F_WIDTH), 0) == 0
    half = radix * DFT_ROWS
    for k1 in range(radix):
        a = ab_ref[k1, :, 0:F_WIDTH]
        b = ab_ref[k1, :, F_WIDTH:]
        p = _dot(cs_ref[:, 0:DFT_N], a)
        q = _dot(cs_ref[:, DFT_N:], b)
        direct = (p + q) * scale
        edge = (_dot(edge_ref[:, 0:DFT_N], a) + _dot(edge_ref[:, DFT_N:], b)) * scale
        mirror = _dot(mir_ref[...], ((p - q) * scale).astype(BF16))
        mirror = jnp.where(first_row, jnp.where(j == 0, edge[1:2], edge[0:1]), mirror)
        for c in range(F_WIDTH // LANES):
            cols = slice(c * LANES, (c + 1) * LANES)
            o_ref[c, pl.ds(k1, DFT_ROWS, stride=radix), :] = direct[:, cols]
            o_ref[c, pl.ds(half + k1, DFT_ROWS, stride=radix), :] = mirror[:, cols]


def _seqdft_block(token_tile, tiles_per_batch, radix):
    b = token_tile // tiles_per_batch
    it = token_tile % tiles_per_batch
    ft = it // radix
    j = jnp.where(ft < 2, ft, 3 - ft)
    return ((b * 2 + j) * 2 + ft // 2) * radix + it % radix


def _seqdft(ab, cs, edge, mir, batch, radix):
    scale = float(1.0 / np.sqrt(radix * DFT_N))
    nj = DFT_N // (2 * DFT_ROWS)
    full = lambda a: pl.BlockSpec(a.shape, lambda b, j: (0,) * a.ndim)
    return pl.pallas_call(
        functools.partial(_seqdft_body, radix=radix, scale=scale),
        grid=(batch, nj),
        in_specs=[pl.BlockSpec((DFT_ROWS, 2 * DFT_N), lambda b, j: (j, 0)), full(edge), full(mir),
                  pl.BlockSpec((None, radix, DFT_N, 2 * F_WIDTH), lambda b, j: (b, 0, 0, 0),
                               pipeline_mode=pl.Buffered(1))],
        out_specs=pl.BlockSpec((F_WIDTH // LANES, 2 * radix * DFT_ROWS, LANES), lambda b, j: (0, b * nj + j, 0)),
        out_shape=jax.ShapeDtypeStruct((F_WIDTH // LANES, batch * radix * DFT_N, LANES), F32),
        compiler_params=pltpu.CompilerParams(dimension_semantics=("arbitrary", "arbitrary"),
                                             vmem_limit_bytes=VMEM_LIMIT),
        name="seqdft",
    )(cs, edge, mir, ab)


def _first_index(hit, rows):
    return jnp.min(jnp.where(hit, rows.astype(F32), 1e6), axis=0, keepdims=True).astype(jnp.int32)


def _postmix_body(of_ref, ob_ref, r_ref, fo_ref, x_ref, wo_ref, gg_ref, g2_ref, wr_ref,
                  h_ref, xn_ref, eidx_ref, cw_ref, rank_ref, cnt_ref, carry_ref):
    @pl.when(pl.program_id(0) == 0)
    def _():
        carry_ref[...] = jnp.zeros_like(carry_ref)

    o = of_ref[...].astype(F32) + ob_ref[...].astype(F32)
    r = r_ref[...].astype(F32)
    parts = []
    for hd in range(GLA_HEADS):
        sl = slice(hd * GLA_DV, (hd + 1) * GLA_DV)
        oh = o[:, sl]
        oh = oh * lax.rsqrt(jnp.mean(oh * oh, axis=-1, keepdims=True) + EPS)
        rh = r[:, sl]
        parts.append((oh * gg_ref[...] * (rh * jax.nn.sigmoid(rh))).astype(BF16))
    on = jnp.concatenate(parts, axis=1)
    fo = jnp.concatenate([fo_ref[c] for c in range(F_WIDTH // LANES)], axis=1)
    mixed = _dot(on, wo_ref[0:GLA_WIDTH, :]) + _dot(fo.astype(BF16), wo_ref[GLA_WIDTH:, :])
    h = x_ref[...] + mixed
    h_ref[...] = h
    xn = h * lax.rsqrt(jnp.mean(h * h, axis=-1, keepdims=True) + EPS) * g2_ref[...]
    _store_token_tiles(xn_ref, xn)
    logits = _dot_nt(wr_ref[...], xn.astype(BF16))

    sub8 = lax.broadcasted_iota(jnp.int32, (8, TM_TOK), 0)
    lg = jnp.where(sub8 < N_GROUPS, logits[0:8], -jnp.inf)
    gmax = jnp.max(lg, axis=0, keepdims=True)
    g_w = 1.0 / jnp.sum(jnp.exp(lg - gmax), axis=0, keepdims=True)
    g_sel = _first_index(lg == gmax, sub8)
    sel = logits[8:16]
    for g in range(1, N_GROUPS):
        sel = jnp.where(g_sel == g, logits[8 + 8 * g:16 + 8 * g], sel)
    m1 = jnp.max(sel, axis=0, keepdims=True)
    i1 = _first_index(sel == m1, sub8)
    sel2 = jnp.where(sub8 == i1, -jnp.inf, sel)
    m2 = jnp.max(sel2, axis=0, keepdims=True)
    i2 = _first_index(sel2 == m2, sub8)
    e21 = jnp.exp(m2 - m1)
    w1 = 1.0 / (1.0 + e21)
    w2 = e21 / (1.0 + e21)
    e1 = g_sel * EXPERTS_PER_GROUP + i1
    e2 = g_sel * EXPERTS_PER_GROUP + i2
    eidx_ref[...] = jnp.concatenate([e1, e2], axis=0)
    cw_ref[...] = jnp.concatenate([g_w * w1, g_w * w2], axis=0)

    sub = lax.broadcasted_iota(jnp.int32, (N_EXPERTS, TM_TOK), 0)
    oh1 = sub == e1
    oh2 = sub == e2
    oh1b = jnp.where(oh1, 1.0, 0.0).astype(BF16)
    oh2b = jnp.where(oh2, 1.0, 0.0).astype(BF16)
    before = (lax.broadcasted_iota(jnp.int32, (TM_TOK, TM_TOK), 0)
              < lax.broadcasted_iota(jnp.int32, (TM_TOK, TM_TOK), 1))
    before = jnp.where(before, 1.0, 0.0).astype(BF16)
    ones = jnp.ones((TM_TOK, LANES), BF16)
    p1 = _dot(oh1b, before)
    p2 = _dot(oh2b, before)
    c1 = _dot(oh1b, ones)
    c2 = _dot(oh2b, ones)
    carry = carry_ref[...]
    rep = TM_TOK // LANES
    base1 = jnp.concatenate([carry] * rep, axis=1)
    base2 = jnp.concatenate([carry + c1] * rep, axis=1)
    rk1 = jnp.sum(jnp.where(oh1, p1 + base1, 0.0), axis=0, keepdims=True)
    rk2 = jnp.sum(jnp.where(oh2, p2 + base2, 0.0), axis=0, keepdims=True)
    rank_ref[...] = jnp.concatenate([rk1, rk2], axis=0).astype(jnp.int32)
    carry = carry + c1 + c2
    carry_ref[...] = carry
    cnt_ref[...] = carry


def _postmix(of, ob, r, fo, fo_block, x2, wo, gg, g2, wr):
    t = x2.shape[0]
    nt = t // TM_TOK
    tok = lambda w: pl.BlockSpec((TM_TOK, w), lambda i: (i, 0))
    full = lambda a: pl.BlockSpec(a.shape, lambda i: (0,) * a.ndim)
    lane2 = pl.BlockSpec((2, TM_TOK), lambda i: (0, i))
    return pl.pallas_call(
        _postmix_body,
        grid=(nt,),
        in_specs=[tok(GLA_WIDTH), tok(GLA_WIDTH), tok(GLA_WIDTH),
                  pl.BlockSpec((F_WIDTH // LANES, TM_TOK, LANES), lambda i: (0, fo_block(i), 0)), tok(D_MODEL),
                  full(wo), full(gg), full(g2), full(wr)],
        out_specs=[tok(D_MODEL), pl.BlockSpec((TM_TOK * TOK_ROWS, LANES), lambda i: (i, 0)), lane2, lane2, lane2,
                   pl.BlockSpec((N_EXPERTS, LANES), lambda i: (0, 0))],
        out_shape=[jax.ShapeDtypeStruct((t, D_MODEL), F32), jax.ShapeDtypeStruct((t * TOK_ROWS, LANES), F32),
                   jax.ShapeDtypeStruct((2, t), jnp.int32), jax.ShapeDtypeStruct((2, t), F32),
                   jax.ShapeDtypeStruct((2, t), jnp.int32), jax.ShapeDtypeStruct((N_EXPERTS, LANES), F32)],
        scratch_shapes=[pltpu.VMEM((N_EXPERTS, LANES), F32)],
        compiler_params=pltpu.CompilerParams(dimension_semantics=("arbitrary",), vmem_limit_bytes=VMEM_LIMIT),
        name="postmix",
    )(of, ob, r, fo, x2, wo, gg, g2, wr)


def _store_token_tiles(ref, val):
    n = val.shape[0]
    for c in range(TOK_ROWS):
        ref[pl.ds(c, n, stride=TOK_ROWS), :] = val[:, c * LANES:(c + 1) * LANES]


def _load_token_tiles(ref, n):
    return jnp.concatenate([ref[pl.ds(c, n, stride=TOK_ROWS), :] for c in range(TOK_ROWS)], axis=1)


def _token_rows(tok):
    if isinstance(tok, int):
        return pl.ds(tok * TOK_ROWS, TOK_ROWS)
    return pl.ds(pl.multiple_of(tok * TOK_ROWS, TOK_ROWS), TOK_ROWS)


def _row_copy(src_hbm, dst_hbm, src_tok, dst_tok, sem):
    return pltpu.make_async_copy(src_hbm.at[_token_rows(src_tok)], dst_hbm.at[_token_rows(dst_tok)], sem)


def _dispatch_body(fill_ref, pos_ref, xn_ref, xs_hbm, zeros_ref, fill_sem, row_sem):
    i = pl.program_id(0)

    @pl.when(i == 0)
    def _():
        zeros_ref[...] = jnp.zeros_like(zeros_ref)
        tile_rows = TM_EXP * TOK_ROWS

        def fill(e):
            rows = pl.ds(pl.multiple_of(fill_ref[e] * tile_rows, tile_rows), tile_rows)
            return pltpu.make_async_copy(zeros_ref, xs_hbm.at[rows], fill_sem)

        for e in range(N_FILL):
            @pl.when(fill_ref[e] >= 0)
            def _():
                fill(e).start()
        for e in range(N_FILL):
            @pl.when(fill_ref[e] >= 0)
            def _():
                fill(e).wait()

    for j in range(TM_DISP):
        for s in range(2):
            _row_copy(xn_ref, xs_hbm, j, pos_ref[s, j], row_sem).start(priority=s)
    for j in range(TM_DISP):
        for s in range(2):
            _row_copy(xn_ref, xs_hbm, j, 0, row_sem).wait()


def _dispatch(xn, pos, fill_rows, rows_total):
    t = xn.shape[0] // TOK_ROWS
    grid_spec = pltpu.PrefetchScalarGridSpec(
        num_scalar_prefetch=1,
        grid=(t // TM_DISP,),
        in_specs=[pl.BlockSpec((2, TM_DISP), lambda i, fill: (0, i), memory_space=pltpu.SMEM),
                  pl.BlockSpec((TM_DISP * TOK_ROWS, LANES), lambda i, fill: (i, 0))],
        out_specs=pl.BlockSpec(memory_space=pl.ANY),
        scratch_shapes=[pltpu.VMEM((TM_EXP * TOK_ROWS, LANES), F32), pltpu.SemaphoreType.DMA(()),
                        pltpu.SemaphoreType.DMA(())],
    )
    return pl.pallas_call(
        _dispatch_body,
        grid_spec=grid_spec,
        out_shape=jax.ShapeDtypeStruct((rows_total * TOK_ROWS, LANES), F32),
        compiler_params=pltpu.CompilerParams(dimension_semantics=("arbitrary",), vmem_limit_bytes=VMEM_LIMIT),
        name="dispatch",
    )(fill_rows, pos, xn)


def _experts_body(te_ref, na_ref, xs_ref, wg_ref, wu_ref, wd_ref, ys_ref):
    @pl.when(pl.program_id(0) < na_ref[0])
    def _():
        x = _load_token_tiles(xs_ref, TM_EXP).astype(BF16)
        gate = _dot(x, wg_ref[...])
        up = _dot(x, wu_ref[...])
        hid = (gate * jax.nn.sigmoid(gate) * up).astype(BF16)
        _store_token_tiles(ys_ref, _dot(hid, wd_ref[...]))

    @pl.when(pl.program_id(0) >= na_ref[0])
    def _():
        ys_ref[...] = jnp.zeros_like(ys_ref)


def _experts(xs, tile_expert, n_active, wg, wu, wd):
    n_tiles = xs.shape[0] // (TM_EXP * TOK_ROWS)
    row = lambda i, te, na: (i, 0)
    wsel = lambda i, te, na: (te[jnp.minimum(i, na[0] - 1)], 0, 0)
    grid_spec = pltpu.PrefetchScalarGridSpec(
        num_scalar_prefetch=2,
        grid=(n_tiles,),
        in_specs=[pl.BlockSpec((TM_EXP * TOK_ROWS, LANES), row),
                  pl.BlockSpec((None, D_MODEL, D_EXPERT), wsel),
                  pl.BlockSpec((None, D_MODEL, D_EXPERT), wsel),
                  pl.BlockSpec((None, D_EXPERT, D_MODEL), wsel)],
        out_specs=pl.BlockSpec((TM_EXP * TOK_ROWS, LANES), row),
    )
    return pl.pallas_call(
        _experts_body,
        grid_spec=grid_spec,
        out_shape=jax.ShapeDtypeStruct(xs.shape, F32),
        compiler_params=pltpu.CompilerParams(dimension_semantics=("arbitrary",), vmem_limit_bytes=VMEM_LIMIT),
        name="experts",
    )(tile_expert, n_active, xs, wg, wu, wd)


def _combine_body(pos_ref, nxt_ref, ys_hbm, cw_ref, h_ref, gf_ref, y_ref, buf_ref, sem):
    i = pl.program_id(0)
    slot = lax.rem(i, 2)

    def row_gather(p_ref, sl, s, j):
        return pltpu.make_async_copy(ys_hbm.at[_token_rows(p_ref[s, j])], buf_ref.at[sl, s, _token_rows(j)],
                                     sem.at[sl])

    def issue_tile(p_ref, sl):
        for j in range(TM_COMB):
            for s in range(2):
                row_gather(p_ref, sl, s, j).start(priority=s)

    @pl.when(i == 0)
    def _():
        issue_tile(pos_ref, slot)

    @pl.when(i + 1 < pl.num_programs(0))
    def _():
        issue_tile(nxt_ref, 1 - slot)

    for j in range(TM_COMB):
        for s in range(2):
            pltpu.make_async_copy(ys_hbm.at[_token_rows(0)], buf_ref.at[slot, s, _token_rows(j)], sem.at[slot]).wait()

    cw = cw_ref[...]
    y0 = _load_token_tiles(buf_ref.at[slot, 0], TM_COMB)
    y1 = _load_token_tiles(buf_ref.at[slot, 1], TM_COMB)
    h = h_ref[...] + cw[:, 0:1] * y0 + cw[:, 1:2] * y1
    y_ref[...] = h * lax.rsqrt(jnp.mean(h * h, axis=-1, keepdims=True) + EPS) * gf_ref[...]


def _combine(ys, pos, cw_t, h, gf):
    t = h.shape[0]
    n = t // TM_COMB
    tok = pl.BlockSpec((TM_COMB, D_MODEL), lambda i: (i, 0))
    return pl.pallas_call(
        _combine_body,
        grid=(n,),
        in_specs=[pl.BlockSpec((2, TM_COMB), lambda i: (0, i), memory_space=pltpu.SMEM),
                  pl.BlockSpec((2, TM_COMB), lambda i: (0, jnp.minimum(i + 1, n - 1)), memory_space=pltpu.SMEM),
                  pl.BlockSpec(memory_space=pl.ANY),
                  pl.BlockSpec((TM_COMB, 2), lambda i: (i, 0)),
                  tok,
                  pl.BlockSpec((1, D_MODEL), lambda i: (0, 0))],
        out_specs=tok,
        out_shape=jax.ShapeDtypeStruct((t, D_MODEL), F32),
        scratch_shapes=[pltpu.VMEM((2, 2, TM_COMB * TOK_ROWS, LANES), F32), pltpu.SemaphoreType.DMA((2,))],
        compiler_params=pltpu.CompilerParams(dimension_semantics=("arbitrary",), vmem_limit_bytes=VMEM_LIMIT),
        name="combine",
    )(pos, pos, ys, cw_t, h, gf)


def _prepare_params(norm1_g, w_in, w_gk2_f, b_gk_f, w_gk2_b, b_gk_b, gla_norm_g, w_out,
                    norm2_g, w_group, w_expert, w_gate, w_up, w_down, norm_f_g):
    w = w_in[0] * norm1_g[0][:, None]
    gate_lo = 2 * GLA_KEY_WIDTH + 2 * GLA_WIDTH
    gate_hi = gate_lo + 2 * GATE_RANK
    w_in_r = jnp.concatenate([w[:, :gate_lo], w[:, gate_hi:], w[:, gate_lo:gate_hi],
                              jnp.zeros((D_MODEL, GATE_COLS - 2 * GATE_RANK), F32)], axis=1).astype(BF16)
    zk = jnp.zeros((GATE_RANK, GLA_KEY_WIDTH), F32)
    wg = jnp.concatenate([jnp.concatenate([w_gk2_f[0], zk], axis=1), jnp.concatenate([zk, w_gk2_b[0]], axis=1),
                          jnp.zeros((GATE_COLS - 2 * GATE_RANK, 2 * GLA_KEY_WIDTH), F32)], axis=0).astype(BF16)
    bg = jnp.concatenate([b_gk_f[0], b_gk_b[0]])[None, :]
    wr = jnp.concatenate([w_group[0].T, jnp.zeros((8 - N_GROUPS, D_MODEL), F32), w_expert[0].T,
                          jnp.zeros((ROUTER_ROWS - 8 - N_EXPERTS, D_MODEL), F32)], axis=0).astype(BF16)
    return dict(
        w_in_r=w_in_r, wg=wg, bg=bg,
        gg=gla_norm_g[0][None, :], wo=w_out[0].astype(BF16), g2=norm2_g[0][None, :], wr=wr,
        w_gate=w_gate[0].astype(BF16), w_up=w_up[0].astype(BF16), w_down=w_down[0].astype(BF16),
        gf=norm_f_g[None, :],
        trif=jnp.asarray(_TRI_F).astype(BF16), trib=jnp.asarray(_TRI_B).astype(BF16),
        chan_dft=jnp.asarray(_CHAN_DFT).astype(BF16), seq_dft=jnp.asarray(_SEQ_DFT_LOWER).astype(BF16),
        seq_edge=jnp.asarray(_SEQ_DFT_EDGE).astype(BF16), seq_mirror=jnp.asarray(_SEQ_DFT_MIRROR).astype(BF16))


def _encoder(x, p):
    batch, seq, _ = x.shape
    t = batch * seq
    assert seq % DFT_N == 0 and seq // DFT_N in (1, 4), "sequence DFT supports seq = 2048 or 8192"
    radix = seq // DFT_N
    x2 = x.reshape(t, D_MODEL)
    qf, kf, tf, qb, kb, tb, v, r, u, decf, decb = _inproj(x2, p["w_in_r"], p["wg"], p["bg"],
                                                          p["trif"], p["trib"], p["chan_dft"])
    of, ob = _gla(qf, kf, tf, qb, kb, tb, v, decf, decb, batch, seq)
    if radix == 1:
        ab = u.reshape(batch, 1, DFT_N, 2 * F_WIDTH)
    else:
        ab = _radix4(u, batch)
    fo = _seqdft(ab, p["seq_dft"], p["seq_edge"], p["seq_mirror"], batch, radix)
    fo_block = functools.partial(_seqdft_block, tiles_per_batch=seq // TM_TOK, radix=radix)
    h, xn, eidx, cw, rank, cnt = _postmix(of, ob, r, fo, fo_block, x2, p["wo"], p["gg"], p["g2"], p["wr"])

    counts = cnt[:, 0].astype(jnp.int32)
    tiles = (counts + TM_EXP - 1) // TM_EXP
    tile_end = jnp.cumsum(tiles)
    tile_start = tile_end - tiles
    experts = jnp.arange(N_EXPERTS, dtype=jnp.int32)
    seg_row = jnp.sum(jnp.where(eidx[:, :, None] == experts, tile_start * TM_EXP, 0), axis=-1)
    pos = seg_row + rank
    n_tiles = 2 * t // TM_EXP + N_EXPERTS
    tile_ids = jnp.arange(n_tiles, dtype=jnp.int32)
    tile_expert = jnp.minimum(jnp.sum((tile_end[None, :] <= tile_ids[:, None]).astype(jnp.int32), axis=1),
                              N_EXPERTS - 1)
    n_active = tile_end[-1:].astype(jnp.int32)
    tail = n_active + jnp.arange(N_EXPERTS, dtype=jnp.int32)
    fill_tiles = jnp.concatenate([jnp.where(tiles > 0, tile_end - 1, -1),
                                  jnp.where(tail < n_tiles, tail, -1)]).astype(jnp.int32)

    xs = _dispatch(xn, pos, fill_tiles, n_tiles * TM_EXP)
    ys = _experts(xs, tile_expert, n_active, p["w_gate"], p["w_up"], p["w_down"])
    y = _combine(ys, pos, cw.T, h, p["gf"])
    return y.reshape(batch, seq, D_MODEL)


def kernel(x_prompt, x_sample, norm1_g, w_in, w_gk2_f, b_gk_f, w_gk2_b, b_gk_b, gla_norm_g, w_out, norm2_g,
           w_group, w_expert, w_gate, w_up, w_down, norm_f_g):
    p = _prepare_params(norm1_g, w_in, w_gk2_f, b_gk_f, w_gk2_b, b_gk_b, gla_norm_g, w_out,
                        norm2_g, w_group, w_expert, w_gate, w_up, w_down, norm_f_g)
    return (_encoder(x_prompt, p), _encoder(x_sample, p))
```

```python
import functools

import numpy as np
import jax
import jax.numpy as jnp
from jax import lax
from jax.experimental import pallas as pl
from jax.experimental.pallas import tpu as pltpu

D_MODEL = 1024
EPS = 1e-6
GLA_HEADS = 4
GLA_DV = 128
GLA_DK = 64
GLA_WIDTH = GLA_HEADS * GLA_DV
GLA_KEY_WIDTH = GLA_HEADS * GLA_DK
GATE_RANK = 16
GATE_NORMALIZER = 16.0
CHUNK = 64
F_GROUPS = 4
F_GROUP_DIM = 128
F_WIDTH = F_GROUPS * F_GROUP_DIM
N_GROUPS = 4
EXPERTS_PER_GROUP = 8
N_EXPERTS = N_GROUPS * EXPERTS_PER_GROUP
D_EXPERT = 256

LANES = 128
V7X_VMEM_BYTES = 64 * 1024 * 1024
VMEM_LIMIT = 56 * 1024 * 1024

TM_TOK = 512
SUB = 128
DFT_N = 2048
DFT_ROWS = 512
TM_EXP = 512
TM_DISP = 512
TM_COMB = 256
GLA_LOOKAHEAD = 2
ROUTER_ROWS = 48
GATE_COLS = 128
IN_COLS_PAD = 2048 + GATE_COLS
TOK_ROWS = D_MODEL // LANES
N_FILL = 2 * N_EXPERTS

BF16 = jnp.bfloat16
F32 = jnp.float32


def _dot(a, b):
    return jnp.dot(a, b, preferred_element_type=F32)


def _interleave(*stages):
    live = list(stages)
    while live:
        for g in list(live):
            try:
                next(g)
            except StopIteration:
                live.remove(g)


def _dot_nt(a, b):
    return lax.dot_general(a, b, (((1,), (1,)), ((), ())), preferred_element_type=F32)


def _dot_tn(a, b):
    return lax.dot_general(a, b, (((0,), (0,)), ((), ())), preferred_element_type=F32)


def _tri_tables():
    r = np.arange(SUB)
    same = (r[:, None] // CHUNK) == (r[None, :] // CHUNK)
    l_incl = same & (r[None, :] <= r[:, None])
    u_strict = same & (r[None, :] > r[:, None])
    u_incl = same & (r[None, :] >= r[:, None])
    l_strict = same & (r[None, :] < r[:, None])
    fwd = np.concatenate([l_incl, u_strict], 0).astype(np.float32)
    bwd = np.concatenate([u_incl, l_strict], 0).astype(np.float32)
    return fwd, bwd


def _chan_dft_table():
    c = np.arange(F_GROUP_DIM)
    ang = 2.0 * np.pi * ((c[:, None] * c[None, :]) % F_GROUP_DIM) / F_GROUP_DIM
    s = 1.0 / np.sqrt(F_GROUP_DIM)
    return np.concatenate([np.cos(ang) * s, -np.sin(ang) * s], 1).astype(np.float32)


def _seq_dft_tables():
    k = np.arange(DFT_N // 2, dtype=np.int64)
    s = np.arange(DFT_N, dtype=np.int64)
    ang = 2.0 * np.pi * ((k[:, None] * s[None, :]) % DFT_N) / DFT_N
    lower = np.concatenate([np.cos(ang), np.sin(ang)], 1).astype(np.float32)
    edge = np.zeros((16, 2 * DFT_N), np.float32)
    edge[0, :DFT_N] = 1.0 - 2.0 * (s % 2)
    edge[1, :DFT_N] = np.array([1.0, 0.0, -1.0, 0.0])[s % 4]
    edge[1, DFT_N:] = -np.array([0.0, 1.0, 0.0, -1.0])[s % 4]
    i = np.arange(DFT_ROWS)
    mirror = ((i[:, None] >= 1) & (i[None, :] == DFT_ROWS - i[:, None])).astype(np.float32)
    return lower, edge, mirror


def _twiddle_tables(radix):
    k1 = np.arange(radix, dtype=np.int64)[:, None]
    s2 = np.arange(DFT_N, dtype=np.int64)[None, :]
    ang = 2.0 * np.pi * ((k1 * s2) % (radix * DFT_N)) / (radix * DFT_N)
    c = np.repeat(np.cos(ang)[:, :, None], LANES, 2).astype(np.float32)
    s = np.repeat(np.sin(ang)[:, :, None], LANES, 2).astype(np.float32)
    return c, s


_TRI_F, _TRI_B = _tri_tables()
_CHAN_DFT = _chan_dft_table()
_SEQ_DFT_LOWER, _SEQ_DFT_EDGE, _SEQ_DFT_MIRROR = _seq_dft_tables()


def _inproj_project(x_ref, w_ref, v_ref, r_ref, qk_s, fx_s, gt_s):
    x = x_ref[...]
    inv = lax.rsqrt(jnp.mean(x * x, axis=-1, keepdims=True) + EPS)
    xb = x.astype(BF16)
    yield
    qk_s[...] = _dot(xb, w_ref[:, 0:2 * GLA_KEY_WIDTH]) * inv
    yield
    v_ref[...] = (_dot(xb, w_ref[:, 512:1024]) * inv).astype(BF16)
    yield
    r_ref[...] = (_dot(xb, w_ref[:, 1024:1536]) * inv).astype(BF16)
    yield
    fx_s[...] = (_dot(xb, w_ref[:, 1536:2048]) * inv).astype(BF16)
    yield
    gt_s[...] = (_dot(xb, w_ref[:, 2048:IN_COLS_PAD]) * inv).astype(BF16)


def _inproj_decay(qk_s, fx_s, gt_s, wg_ref, bg_ref, trif_ref, trib_ref, cs_ref,
                  qf_ref, kf_ref, tf_ref, qb_ref, kb_ref, tb_ref, u_ref, decf_ref, decb_ref, tot_ref):
    z = _dot(gt_s[...], wg_ref[...]) + bg_ref[...]
    la = (jnp.minimum(z, 0.0) - jnp.log1p(jnp.exp(-jnp.abs(z)))) * (1.0 / GATE_NORMALIZER)
    la_hi = la.astype(BF16)
    la_lo = (la - la_hi.astype(F32)).astype(BF16)
    trif = trif_ref[...]
    trib = trib_ref[...]
    scale = GLA_DK ** -0.5
    for s in range(TM_TOK // SUB):
        yield
        rows = slice(s * SUB, (s + 1) * SUB)
        q = qk_s[rows, 0:GLA_KEY_WIDTH]
        k = qk_s[rows, GLA_KEY_WIDTH:2 * GLA_KEY_WIDTH]
        rf = _dot(trif, la_hi[rows, 0:GLA_KEY_WIDTH]) + _dot(trif, la_lo[rows, 0:GLA_KEY_WIDTH])
        b, tl = rf[0:SUB], rf[SUB:2 * SUB]
        qf_ref[rows, :] = (q * scale * jnp.exp(b)).astype(BF16)
        kf_ref[rows, :] = (k * jnp.exp(-b)).astype(BF16)
        tf_ref[rows, :] = (k * jnp.exp(tl)).astype(BF16)
        totf = b + tl
        tot_ref[0, rows, :] = totf[:, 0:LANES]
        tot_ref[1, rows, :] = totf[:, LANES:]
        rb = _dot(trib, la_hi[rows, GLA_KEY_WIDTH:]) + _dot(trib, la_lo[rows, GLA_KEY_WIDTH:])
        c, tlb = rb[0:SUB], rb[SUB:2 * SUB]
        qb_ref[rows, :] = (q * scale * jnp.exp(c)).astype(BF16)
        kb_ref[rows, :] = (k * jnp.exp(-c)).astype(BF16)
        tb_ref[rows, :] = (k * jnp.exp(tlb)).astype(BF16)
        totb = c + tlb
        tot_ref[2, rows, :] = totb[:, 0:LANES]
        tot_ref[3, rows, :] = totb[:, LANES:]
    yield
    chunk_rows = pl.ds(0, TM_TOK // CHUNK, stride=CHUNK)
    decf_ref[:, 0:LANES] = jnp.exp(tot_ref[0, chunk_rows, :])
    decf_ref[:, LANES:] = jnp.exp(tot_ref[1, chunk_rows, :])
    decb_ref[:, 0:LANES] = jnp.exp(tot_ref[2, chunk_rows, :])
    decb_ref[:, LANES:] = jnp.exp(tot_ref[3, chunk_rows, :])
    cs = cs_ref[...]
    for g in range(F_GROUPS):
        res = _dot(fx_s[:, g * LANES:(g + 1) * LANES], cs)
        u_ref[:, g * LANES:(g + 1) * LANES] = res[:, 0:LANES].astype(BF16)
        u_ref[:, F_WIDTH + g * LANES:F_WIDTH + (g + 1) * LANES] = res[:, LANES:].astype(BF16)


def _inproj_body(x_ref, w_ref, wg_ref, bg_ref, trif_ref, trib_ref, cs_ref,
                 qf_ref, kf_ref, tf_ref, qb_ref, kb_ref, tb_ref, v_ref, r_ref, u_ref, decf_ref, decb_ref,
                 qk0, qk1, fx0, fx1, gt0, gt1, tot_ref):
    i = pl.program_id(0)

    @pl.when(i == 0)
    def _():
        qk1[...] = jnp.zeros_like(qk1)
        fx1[...] = jnp.zeros_like(fx1)
        gt1[...] = jnp.zeros_like(gt1)

    def step(cur, prev):
        stage_a = _inproj_project(x_ref, w_ref, v_ref, r_ref, *cur)
        stage_b = _inproj_decay(*prev, wg_ref, bg_ref, trif_ref, trib_ref, cs_ref,
                                qf_ref, kf_ref, tf_ref, qb_ref, kb_ref, tb_ref, u_ref, decf_ref, decb_ref, tot_ref)
        _interleave(stage_a, stage_b)

    @pl.when(lax.rem(i, 2) == 0)
    def _():
        step((qk0, fx0, gt0), (qk1, fx1, gt1))

    @pl.when(lax.rem(i, 2) == 1)
    def _():
        step((qk1, fx1, gt1), (qk0, fx0, gt0))


def _inproj(x2, w_in_r, wg, bg, trif, trib, cs):
    t = x2.shape[0]
    nt = t // TM_TOK
    cur = lambda i: (jnp.minimum(i, nt - 1), 0)
    prev = lambda i: (jnp.maximum(i - 1, 0), 0)
    full = lambda a: pl.BlockSpec(a.shape, lambda i: (0,) * a.ndim)
    kw = jax.ShapeDtypeStruct((t, GLA_KEY_WIDTH), BF16)
    dec = jax.ShapeDtypeStruct((t // CHUNK, GLA_KEY_WIDTH), F32)
    dec_spec = pl.BlockSpec((TM_TOK // CHUNK, GLA_KEY_WIDTH), prev)
    wide = jax.ShapeDtypeStruct((t, GLA_WIDTH), BF16)
    return pl.pallas_call(
        _inproj_body,
        grid=(nt + 1,),
        in_specs=[pl.BlockSpec((TM_TOK, D_MODEL), cur), full(w_in_r), full(wg), full(bg),
                  full(trif), full(trib), full(cs)],
        out_specs=[pl.BlockSpec((TM_TOK, GLA_KEY_WIDTH), prev)] * 6
                  + [pl.BlockSpec((TM_TOK, GLA_WIDTH), cur), pl.BlockSpec((TM_TOK, GLA_WIDTH), cur),
                     pl.BlockSpec((TM_TOK, 2 * F_WIDTH), prev), dec_spec, dec_spec],
        out_shape=[kw] * 6 + [wide, wide, jax.ShapeDtypeStruct((t, 2 * F_WIDTH), BF16), dec, dec],
        scratch_shapes=[pltpu.VMEM((TM_TOK, 2 * GLA_KEY_WIDTH), F32)] * 2
                       + [pltpu.VMEM((TM_TOK, F_WIDTH), BF16)] * 2
                       + [pltpu.VMEM((TM_TOK, GATE_COLS), BF16)] * 2
                       + [pltpu.VMEM((2 * GLA_KEY_WIDTH // LANES, TM_TOK, LANES), F32)],
        compiler_params=pltpu.CompilerParams(dimension_semantics=("arbitrary",), vmem_limit_bytes=VMEM_LIMIT),
        name="inproj",
    )(x2, w_in_r, wg, bg, trif, trib, cs)


def _gla_local(q_ref, k_ref, t_ref, v_ref, c, p, causal, m_lo, mv_lo, bd):
    rows = slice(c * CHUNK, (c + 1) * CHUNK)
    kl = slice(p * LANES, (p + 1) * LANES)
    vl = slice(p * 2 * GLA_DV, (p + 1) * 2 * GLA_DV)
    qd = q_ref[rows, kl]
    kd = k_ref[rows, kl]
    kt = t_ref[rows, kl]
    vv = v_ref[rows, vl]
    zk = jnp.zeros_like(kd)
    zv = jnp.zeros_like(vv)
    kbd = jnp.concatenate([jnp.where(m_lo, kd, zk), jnp.where(m_lo, zk, kd)], axis=0)
    att = _dot_nt(qd, kbd)
    att = jnp.where(causal, att, 0.0).astype(BF16)
    vbd = jnp.concatenate([jnp.where(mv_lo, vv, zv), jnp.where(mv_lo, zv, vv)], axis=0)
    kv = jnp.where(bd, _dot_tn(vv, kt), 0.0)
    return qd, att, vbd, kv


def _gla_body(qf_ref, kf_ref, tf_ref, vf_ref, df_ref, qb_ref, kb_ref, tb_ref, vb_ref, db_ref,
              of_ref, ob_ref, sf_ref, sb_ref):
    @pl.when(pl.program_id(1) == 0)
    def _():
        sf_ref[...] = jnp.zeros_like(sf_ref)
        sb_ref[...] = jnp.zeros_like(sb_ref)

    lane = lax.broadcasted_iota(jnp.int32, (CHUNK, LANES), 1)
    row = lax.broadcasted_iota(jnp.int32, (CHUNK, LANES), 0)
    m_lo = lane < GLA_DK
    col = lane & (CHUNK - 1)
    causal_f = row >= col
    causal_b = row <= col
    mv_lo = lax.broadcasted_iota(jnp.int32, (CHUNK, 2 * GLA_DV), 1) < GLA_DV
    bd = ((lax.broadcasted_iota(jnp.int32, (2 * GLA_DV, LANES), 0) < GLA_DV)
          == (lax.broadcasted_iota(jnp.int32, (2 * GLA_DV, LANES), 1) < GLA_DK))
    n = TM_TOK // CHUNK
    pairs = range(GLA_HEADS // 2)
    dirs = ((qf_ref, kf_ref, tf_ref, vf_ref, df_ref, of_ref, sf_ref, causal_f, lambda j: j),
            (qb_ref, kb_ref, tb_ref, vb_ref, db_ref, ob_ref, sb_ref, causal_b, lambda j: n - 1 - j))

    def local(j):
        return [[_gla_local(q, k, t, v, order(j), p, causal, m_lo, mv_lo, bd) for p in pairs]
                for (q, k, t, v, _, _, _, causal, order) in dirs]

    state = [[s_ref[p] for p in pairs] for (_, _, _, _, _, _, s_ref, _, _) in dirs]
    ahead = [local(j) for j in range(GLA_LOOKAHEAD)]
    for j in range(n):
        cur = ahead.pop(0)
        if j + GLA_LOOKAHEAD < n:
            ahead.append(local(j + GLA_LOOKAHEAD))
        for d, (_, _, _, _, d_ref, o_ref, _, _, order) in enumerate(dirs):
            c = order(j)
            dec = d_ref[c:c + 1, :]
            for p in pairs:
                qd, att, vbd, kv = cur[d][p]
                st = state[d][p]
                o = _dot(att, vbd) + _dot_nt(qd, st.astype(BF16))
                o_ref[c * CHUNK:(c + 1) * CHUNK, p * 2 * GLA_DV:(p + 1) * 2 * GLA_DV] = o.astype(o_ref.dtype)
                state[d][p] = st * dec[:, p * LANES:(p + 1) * LANES] + kv
    for d, (_, _, _, _, _, _, s_ref, _, _) in enumerate(dirs):
        for p in pairs:
            s_ref[p] = state[d][p]


def _gla(qf, kf, tf, qb, kb, tb, v, decf, decb, batch, seq):
    t = batch * seq
    nt = seq // TM_TOK
    fwd = lambda b, i: (b * nt + i, 0)
    bwd = lambda b, i: (b * nt + nt - 1 - i, 0)
    ks = lambda m: pl.BlockSpec((TM_TOK, GLA_KEY_WIDTH), m)
    vs = lambda m: pl.BlockSpec((TM_TOK, GLA_WIDTH), m)
    ds = lambda m: pl.BlockSpec((TM_TOK // CHUNK, GLA_KEY_WIDTH), m)
    o = jax.ShapeDtypeStruct((t, GLA_WIDTH), BF16)
    state = pltpu.VMEM((GLA_HEADS // 2, 2 * GLA_DV, LANES), F32)
    return pl.pallas_call(
        _gla_body,
        grid=(batch, nt),
        in_specs=[ks(fwd), ks(fwd), ks(fwd), vs(fwd), ds(fwd), ks(bwd), ks(bwd), ks(bwd), vs(bwd), ds(bwd)],
        out_specs=[vs(fwd), vs(bwd)],
        out_shape=[o, o],
        scratch_shapes=[state, state],
        compiler_params=pltpu.CompilerParams(dimension_semantics=("arbitrary", "arbitrary"),
                                             vmem_limit_bytes=VMEM_LIMIT),
        name="gla",
    )(qf, kf, tf, v, decf, qb, kb, tb, v, decb)


RADIX_ROWS = 256


def _radix4_body(z_ref, twc_ref, tws_ref, y_ref):
    z = [z_ref[s].astype(F32) for s in range(4)]
    re = [a[:, 0:F_WIDTH] for a in z]
    im = [a[:, F_WIDTH:] for a in z]
    ar, ai = re[0] + re[2], im[0] + im[2]
    br, bi = re[0] - re[2], im[0] - im[2]
    cr, ci = re[1] + re[3], im[1] + im[3]
    dr, di = re[1] - re[3], im[1] - im[3]
    y = [(ar + cr, ai + ci), (br + di, bi - dr), (ar - cr, ai - ci), (br - di, bi + dr)]
    y_ref[0, :, 0:F_WIDTH] = y[0][0].astype(BF16)
    y_ref[0, :, F_WIDTH:] = y[0][1].astype(BF16)
    for k1 in range(1, 4):
        c = jnp.concatenate([twc_ref[k1]] * (F_WIDTH // LANES), axis=1)
        s = jnp.concatenate([tws_ref[k1]] * (F_WIDTH // LANES), axis=1)
        yr, yi = y[k1]
        y_ref[k1, :, 0:F_WIDTH] = (yr * c + yi * s).astype(BF16)
        y_ref[k1, :, F_WIDTH:] = (yi * c - yr * s).astype(BF16)


def _radix4(u, batch):
    z = u.reshape(batch, 4, DFT_N, 2 * F_WIDTH)
    twc, tws = _twiddle_tables(4)
    twc, tws = jnp.asarray(twc), jnp.asarray(tws)
    nr = DFT_N // RADIX_ROWS
    blk = pl.BlockSpec((None, 4, RADIX_ROWS, 2 * F_WIDTH), lambda b, i: (b, 0, i, 0))
    tw = pl.BlockSpec((4, RADIX_ROWS, LANES), lambda b, i: (0, i, 0))
    y = pl.pallas_call(
        _radix4_body,
        grid=(batch, nr),
        in_specs=[blk, tw, tw],
        out_specs=blk,
        out_shape=jax.ShapeDtypeStruct((batch, 4, DFT_N, 2 * F_WIDTH), BF16),
        compiler_params=pltpu.CompilerParams(dimension_semantics=("arbitrary", "arbitrary"),
                                             vmem_limit_bytes=VMEM_LIMIT),
        name="radix4",
    )(z, twc, tws)
    return y


def _seqdft_body(cs_ref, edge_ref, mir_ref, ab_ref, o_ref, *, radix, scale):
    j = pl.program_id(1)
    first_row = lax.broadcasted_iota(jnp.int32, (DFT_ROWS, F_WIDTH), 0) == 0
    half = radix * DFT_ROWS
    rows = pl.ds(pl.multiple_of(j * DFT_ROWS, DFT_ROWS), DFT_ROWS)
    for k1 in range(radix):
        a = ab_ref[k1, :, 0:F_WIDTH]
        b = ab_ref[k1, :, F_WIDTH:]
        p = _dot(cs_ref[rows, 0:DFT_N], a)
        q = _dot(cs_ref[rows, DFT_N:], b)
        direct = (p + q) * scale
        edge = (_dot(edge_ref[:, 0:DFT_N], a) + _dot(edge_ref[:, DFT_N:], b)) * scale
        mirror = _dot(mir_ref[...], ((p - q) * scale).astype(BF16))
        mirror = jnp.where(first_row, jnp.where(j == 0, edge[1:2], edge[0:1]), mirror)
        for c in range(F_WIDTH // LANES):
            cols = slice(c * LANES, (c + 1) * LANES)
            o_ref[c, pl.ds(k1, DFT_ROWS, stride=radix), :] = direct[:, cols]
            o_ref[c, pl.ds(half + k1, DFT_ROWS, stride=radix), :] = mirror[:, cols]


def _seqdft_block(token_tile, tiles_per_batch, radix):
    b = token_tile // tiles_per_batch
    it = token_tile % tiles_per_batch
    ft = it // radix
    j = jnp.where(ft < 2, ft, 3 - ft)
    return ((b * 2 + j) * 2 + ft // 2) * radix + it % radix


def _seqdft(ab, cs, edge, mir, batch, radix):
    scale = float(1.0 / np.sqrt(radix * DFT_N))
    nj = DFT_N // (2 * DFT_ROWS)
    full = lambda a: pl.BlockSpec(a.shape, lambda b, j: (0,) * a.ndim)
    return pl.pallas_call(
        functools.partial(_seqdft_body, radix=radix, scale=scale),
        grid=(batch, nj),
        in_specs=[pl.BlockSpec(cs.shape, lambda b, j: (0, 0), pipeline_mode=pl.Buffered(1)), full(edge), full(mir),
                  pl.BlockSpec((None, radix, DFT_N, 2 * F_WIDTH), lambda b, j: (b, 0, 0, 0),
                               pipeline_mode=pl.Buffered(1))],
        out_specs=pl.BlockSpec((F_WIDTH // LANES, 2 * radix * DFT_ROWS, LANES), lambda b, j: (0, b * nj + j, 0)),
        out_shape=jax.ShapeDtypeStruct((F_WIDTH // LANES, batch * radix * DFT_N, LANES), F32),
        compiler_params=pltpu.CompilerParams(dimension_semantics=("arbitrary", "arbitrary"),
                                             vmem_limit_bytes=VMEM_LIMIT),
        name="seqdft",
    )(cs, edge, mir, ab)


def _first_index(hit, rows):
    return jnp.min(jnp.where(hit, rows.astype(F32), 1e6), axis=0, keepdims=True).astype(jnp.int32)


def _postmix_body(of_ref, ob_ref, r_ref, fo_ref, x_ref, wo_ref, gg_ref, g2_ref, wr_ref,
                  h_ref, xn_ref, eidx_ref, cw_ref, rank_ref, cnt_ref, carry_ref):
    @pl.when(pl.program_id(0) == 0)
    def _():
        carry_ref[...] = jnp.zeros_like(carry_ref)

    o = of_ref[...].astype(F32) + ob_ref[...].astype(F32)
    r = r_ref[...].astype(F32)
    parts = []
    for hd in range(GLA_HEADS):
        sl = slice(hd * GLA_DV, (hd + 1) * GLA_DV)
        oh = o[:, sl]
        oh = oh * lax.rsqrt(jnp.mean(oh * oh, axis=-1, keepdims=True) + EPS)
        rh = r[:, sl]
        parts.append((oh * gg_ref[...] * (rh * jax.nn.sigmoid(rh))).astype(BF16))
    on = jnp.concatenate(parts, axis=1)
    fo = jnp.concatenate([fo_ref[c] for c in range(F_WIDTH // LANES)], axis=1)
    mixed = _dot(on, wo_ref[0:GLA_WIDTH, :]) + _dot(fo.astype(BF16), wo_ref[GLA_WIDTH:, :])
    h = x_ref[...] + mixed
    h_ref[...] = h
    xn = h * lax.rsqrt(jnp.mean(h * h, axis=-1, keepdims=True) + EPS) * g2_ref[...]
    _store_token_tiles(xn_ref, xn)
    logits = _dot_nt(wr_ref[...], xn.astype(BF16))

    sub8 = lax.broadcasted_iota(jnp.int32, (8, TM_TOK), 0)
    lg = jnp.where(sub8 < N_GROUPS, logits[0:8], -jnp.inf)
    gmax = jnp.max(lg, axis=0, keepdims=True)
    g_w = 1.0 / jnp.sum(jnp.exp(lg - gmax), axis=0, keepdims=True)
    g_sel = _first_index(lg == gmax, sub8)
    sel = logits[8:16]
    for g in range(1, N_GROUPS):
        sel = jnp.where(g_sel == g, logits[8 + 8 * g:16 + 8 * g], sel)
    m1 = jnp.max(sel, axis=0, keepdims=True)
    i1 = _first_index(sel == m1, sub8)
    sel2 = jnp.where(sub8 == i1, -jnp.inf, sel)
    m2 = jnp.max(sel2, axis=0, keepdims=True)
    i2 = _first_index(sel2 == m2, sub8)
    e21 = jnp.exp(m2 - m1)
    w1 = 1.0 / (1.0 + e21)
    w2 = e21 / (1.0 + e21)
    e1 = g_sel * EXPERTS_PER_GROUP + i1
    e2 = g_sel * EXPERTS_PER_GROUP + i2
    eidx_ref[...] = jnp.concatenate([e1, e2], axis=0)
    cw_ref[...] = jnp.concatenate([g_w * w1, g_w * w2], axis=0)

    sub = lax.broadcasted_iota(jnp.int32, (N_EXPERTS, TM_TOK), 0)
    oh1 = sub == e1
    oh2 = sub == e2
    oh1b = jnp.where(oh1, 1.0, 0.0).astype(BF16)
    oh2b = jnp.where(oh2, 1.0, 0.0).astype(BF16)
    before = (lax.broadcasted_iota(jnp.int32, (TM_TOK, TM_TOK), 0)
              < lax.broadcasted_iota(jnp.int32, (TM_TOK, TM_TOK), 1))
    before = jnp.where(before, 1.0, 0.0).astype(BF16)
    ones = jnp.ones((TM_TOK, LANES), BF16)
    oh12 = jnp.concatenate([oh1b, oh2b], axis=0)
    p12 = _dot(oh12, before)
    c12 = _dot(oh12, ones)
    p1, p2 = p12[0:N_EXPERTS], p12[N_EXPERTS:]
    c1, c2 = c12[0:N_EXPERTS], c12[N_EXPERTS:]
    carry = carry_ref[...]
    rep = TM_TOK // LANES
    base1 = jnp.concatenate([carry] * rep, axis=1)
    base2 = jnp.concatenate([carry + c1] * rep, axis=1)
    rk1 = jnp.sum(jnp.where(oh1, p1 + base1, 0.0), axis=0, keepdims=True)
    rk2 = jnp.sum(jnp.where(oh2, p2 + base2, 0.0), axis=0, keepdims=True)
    rank_ref[...] = jnp.concatenate([rk1, rk2], axis=0).astype(jnp.int32)
    carry = carry + c1 + c2
    carry_ref[...] = carry
    cnt_ref[...] = carry


def _postmix(of, ob, r, fo, fo_block, x2, wo, gg, g2, wr):
    t = x2.shape[0]
    nt = t // TM_TOK
    tok = lambda w: pl.BlockSpec((TM_TOK, w), lambda i: (i, 0))
    full = lambda a: pl.BlockSpec(a.shape, lambda i: (0,) * a.ndim)
    lane2 = pl.BlockSpec((2, TM_TOK), lambda i: (0, i))
    return pl.pallas_call(
        _postmix_body,
        grid=(nt,),
        in_specs=[tok(GLA_WIDTH), tok(GLA_WIDTH), tok(GLA_WIDTH),
                  pl.BlockSpec((F_WIDTH // LANES, TM_TOK, LANES), lambda i: (0, fo_block(i), 0)), tok(D_MODEL),
                  full(wo), full(gg), full(g2), full(wr)],
        out_specs=[tok(D_MODEL), pl.BlockSpec((TM_TOK * TOK_ROWS, LANES), lambda i: (i, 0)), lane2, lane2, lane2,
                   pl.BlockSpec((N_EXPERTS, LANES), lambda i: (0, 0))],
        out_shape=[jax.ShapeDtypeStruct((t, D_MODEL), F32), jax.ShapeDtypeStruct((t * TOK_ROWS, LANES), F32),
                   jax.ShapeDtypeStruct((2, t), jnp.int32), jax.ShapeDtypeStruct((2, t), F32),
                   jax.ShapeDtypeStruct((2, t), jnp.int32), jax.ShapeDtypeStruct((N_EXPERTS, LANES), F32)],
        scratch_shapes=[pltpu.VMEM((N_EXPERTS, LANES), F32)],
        compiler_params=pltpu.CompilerParams(dimension_semantics=("arbitrary",), vmem_limit_bytes=VMEM_LIMIT),
        name="postmix",
    )(of, ob, r, fo, x2, wo, gg, g2, wr)


def _store_token_tiles(ref, val):
    n = val.shape[0]
    for c in range(TOK_ROWS):
        ref[pl.ds(c, n, stride=TOK_ROWS), :] = val[:, c * LANES:(c + 1) * LANES]


def _load_token_tiles(ref, n):
    return jnp.concatenate([ref[pl.ds(c, n, stride=TOK_ROWS), :] for c in range(TOK_ROWS)], axis=1)


def _token_rows(tok):
    if isinstance(tok, int):
        return pl.ds(tok * TOK_ROWS, TOK_ROWS)
    return pl.ds(pl.multiple_of(tok * TOK_ROWS, TOK_ROWS), TOK_ROWS)


def _row_copy(src_hbm, dst_hbm, src_tok, dst_tok, sem):
    return pltpu.make_async_copy(src_hbm.at[_token_rows(src_tok)], dst_hbm.at[_token_rows(dst_tok)], sem)


def _dispatch_body(fill_ref, pos_ref, xn_ref, xs_hbm, zeros_ref, fill_sem, row_sem):
    i = pl.program_id(0)

    @pl.when(i == 0)
    def _():
        zeros_ref[...] = jnp.zeros_like(zeros_ref)
        tile_rows = TM_EXP * TOK_ROWS

        def fill(e):
            rows = pl.ds(pl.multiple_of(fill_ref[e] * tile_rows, tile_rows), tile_rows)
            return pltpu.make_async_copy(zeros_ref, xs_hbm.at[rows], fill_sem)

        for e in range(N_FILL):
            @pl.when(fill_ref[e] >= 0)
            def _():
                fill(e).start()
        for e in range(N_FILL):
            @pl.when(fill_ref[e] >= 0)
            def _():
                fill(e).wait()

    for j in range(TM_DISP):
        for s in range(2):
            _row_copy(xn_ref, xs_hbm, j, pos_ref[s, j], row_sem).start(priority=s)
    for j in range(TM_DISP):
        for s in range(2):
            _row_copy(xn_ref, xs_hbm, j, 0, row_sem).wait()


def _dispatch(xn, pos, fill_rows, rows_total):
    t = xn.shape[0] // TOK_ROWS
    grid_spec = pltpu.PrefetchScalarGridSpec(
        num_scalar_prefetch=1,
        grid=(t // TM_DISP,),
        in_specs=[pl.BlockSpec((2, TM_DISP), lambda i, fill: (0, i), memory_space=pltpu.SMEM),
                  pl.BlockSpec((TM_DISP * TOK_ROWS, LANES), lambda i, fill: (i, 0))],
        out_specs=pl.BlockSpec(memory_space=pl.ANY),
        scratch_shapes=[pltpu.VMEM((TM_EXP * TOK_ROWS, LANES), F32), pltpu.SemaphoreType.DMA(()),
                        pltpu.SemaphoreType.DMA(())],
    )
    return pl.pallas_call(
        _dispatch_body,
        grid_spec=grid_spec,
        out_shape=jax.ShapeDtypeStruct((rows_total * TOK_ROWS, LANES), F32),
        compiler_params=pltpu.CompilerParams(dimension_semantics=("arbitrary",), vmem_limit_bytes=VMEM_LIMIT),
        name="dispatch",
    )(fill_rows, pos, xn)


def _experts_body(te_ref, na_ref, xs_ref, wg_ref, wu_ref, wd_ref, ys_ref):
    @pl.when(pl.program_id(0) < na_ref[0])
    def _():
        x = _load_token_tiles(xs_ref, TM_EXP).astype(BF16)
        gate = _dot(x, wg_ref[...])
        up = _dot(x, wu_ref[...])
        hid = (gate * jax.nn.sigmoid(gate) * up).astype(BF16)
        _store_token_tiles(ys_ref, _dot(hid, wd_ref[...]))

    @pl.when(pl.program_id(0) >= na_ref[0])
    def _():
        ys_ref[...] = jnp.zeros_like(ys_ref)


def _experts(xs, tile_expert, n_active, wg, wu, wd):
    n_tiles = xs.shape[0] // (TM_EXP * TOK_ROWS)
    row = lambda i, te, na: (i, 0)
    wsel = lambda i, te, na: (te[jnp.minimum(i, na[0] - 1)], 0, 0)
    grid_spec = pltpu.PrefetchScalarGridSpec(
        num_scalar_prefetch=2,
        grid=(n_tiles,),
        in_specs=[pl.BlockSpec((TM_EXP * TOK_ROWS, LANES), row),
                  pl.BlockSpec((None, D_MODEL, D_EXPERT), wsel),
                  pl.BlockSpec((None, D_MODEL, D_EXPERT), wsel),
                  pl.BlockSpec((None, D_EXPERT, D_MODEL), wsel)],
        out_specs=pl.BlockSpec((TM_EXP * TOK_ROWS, LANES), row),
    )
    return pl.pallas_call(
        _experts_body,
        grid_spec=grid_spec,
        out_shape=jax.ShapeDtypeStruct(xs.shape, F32),
        compiler_params=pltpu.CompilerParams(dimension_semantics=("arbitrary",), vmem_limit_bytes=VMEM_LIMIT),
        name="experts",
    )(tile_expert, n_active, xs, wg, wu, wd)


def _combine_body(pos_ref, nxt_ref, ys_hbm, cw_ref, h_ref, gf_ref, y_ref, buf_ref, sem):
    i = pl.program_id(0)
    slot = lax.rem(i, 2)

    def row_gather(p_ref, sl, s, j):
        return pltpu.make_async_copy(ys_hbm.at[_token_rows(p_ref[s, j])], buf_ref.at[sl, s, _token_rows(j)],
                                     sem.at[sl])

    def issue_tile(p_ref, sl):
        for j in range(TM_COMB):
            for s in range(2):
                row_gather(p_ref, sl, s, j).start(priority=s)

    @pl.when(i == 0)
    def _():
        issue_tile(pos_ref, slot)

    @pl.when(i + 1 < pl.num_programs(0))
    def _():
        issue_tile(nxt_ref, 1 - slot)

    for j in range(TM_COMB):
        for s in range(2):
            pltpu.make_async_copy(ys_hbm.at[_token_rows(0)], buf_ref.at[slot, s, _token_rows(j)], sem.at[slot]).wait()

    cw = cw_ref[...]
    y0 = _load_token_tiles(buf_ref.at[slot, 0], TM_COMB)
    y1 = _load_token_tiles(buf_ref.at[slot, 1], TM_COMB)
    h = h_ref[...] + cw[:, 0:1] * y0 + cw[:, 1:2] * y1
    y_ref[...] = h * lax.rsqrt(jnp.mean(h * h, axis=-1, keepdims=True) + EPS) * gf_ref[...]


def _combine(ys, pos, cw_t, h, gf):
    t = h.shape[0]
    n = t // TM_COMB
    tok = pl.BlockSpec((TM_COMB, D_MODEL), lambda i: (i, 0))
    return pl.pallas_call(
        _combine_body,
        grid=(n,),
        in_specs=[pl.BlockSpec((2, TM_COMB), lambda i: (0, i), memory_space=pltpu.SMEM),
                  pl.BlockSpec((2, TM_COMB), lambda i: (0, jnp.minimum(i + 1, n - 1)), memory_space=pltpu.SMEM),
                  pl.BlockSpec(memory_space=pl.ANY),
                  pl.BlockSpec((TM_COMB, 2), lambda i: (i, 0)),
                  tok,
                  pl.BlockSpec((1, D_MODEL), lambda i: (0, 0))],
        out_specs=tok,
        out_shape=jax.ShapeDtypeStruct((t, D_MODEL), F32),
        scratch_shapes=[pltpu.VMEM((2, 2, TM_COMB * TOK_ROWS, LANES), F32), pltpu.SemaphoreType.DMA((2,))],
        compiler_params=pltpu.CompilerParams(dimension_semantics=("arbitrary",), vmem_limit_bytes=VMEM_LIMIT),
        name="combine",
    )(pos, pos, ys, cw_t, h, gf)


def _prepare_params(norm1_g, w_in, w_gk2_f, b_gk_f, w_gk2_b, b_gk_b, gla_norm_g, w_out,
                    norm2_g, w_group, w_expert, w_gate, w_up, w_down, norm_f_g):
    w = w_in[0] * norm1_g[0][:, None]
    gate_lo = 2 * GLA_KEY_WIDTH + 2 * GLA_WIDTH
    gate_hi = gate_lo + 2 * GATE_RANK
    w_in_r = jnp.concatenate([w[:, :gate_lo], w[:, gate_hi:], w[:, gate_lo:gate_hi],
                              jnp.zeros((D_MODEL, GATE_COLS - 2 * GATE_RANK), F32)], axis=1).astype(BF16)
    zk = jnp.zeros((GATE_RANK, GLA_KEY_WIDTH), F32)
    wg = jnp.concatenate([jnp.concatenate([w_gk2_f[0], zk], axis=1), jnp.concatenate([zk, w_gk2_b[0]], axis=1),
                          jnp.zeros((GATE_COLS - 2 * GATE_RANK, 2 * GLA_KEY_WIDTH), F32)], axis=0).astype(BF16)
    bg = jnp.concatenate([b_gk_f[0], b_gk_b[0]])[None, :]
    wr = jnp.concatenate([w_group[0].T, jnp.zeros((8 - N_GROUPS, D_MODEL), F32), w_expert[0].T,
                          jnp.zeros((ROUTER_ROWS - 8 - N_EXPERTS, D_MODEL), F32)], axis=0).astype(BF16)
    return dict(
        w_in_r=w_in_r, wg=wg, bg=bg,
        gg=gla_norm_g[0][None, :], wo=w_out[0].astype(BF16), g2=norm2_g[0][None, :], wr=wr,
        w_gate=w_gate[0].astype(BF16), w_up=w_up[0].astype(BF16), w_down=w_down[0].astype(BF16),
        gf=norm_f_g[None, :],
        trif=jnp.asarray(_TRI_F).astype(BF16), trib=jnp.asarray(_TRI_B).astype(BF16),
        chan_dft=jnp.asarray(_CHAN_DFT).astype(BF16), seq_dft=jnp.asarray(_SEQ_DFT_LOWER).astype(BF16),
        seq_edge=jnp.asarray(_SEQ_DFT_EDGE).astype(BF16), seq_mirror=jnp.asarray(_SEQ_DFT_MIRROR).astype(BF16))


def _encoder(x, p):
    batch, seq, _ = x.shape
    t = batch * seq
    assert seq % DFT_N == 0 and seq // DFT_N in (1, 4), "sequence DFT supports seq = 2048 or 8192"
    radix = seq // DFT_N
    x2 = x.reshape(t, D_MODEL)
    qf, kf, tf, qb, kb, tb, v, r, u, decf, decb = _inproj(x2, p["w_in_r"], p["wg"], p["bg"],
                                                          p["trif"], p["trib"], p["chan_dft"])
    of, ob = _gla(qf, kf, tf, qb, kb, tb, v, decf, decb, batch, seq)
    if radix == 1:
        ab = u.reshape(batch, 1, DFT_N, 2 * F_WIDTH)
    else:
        ab = _radix4(u, batch)
    fo = _seqdft(ab, p["seq_dft"], p["seq_edge"], p["seq_mirror"], batch, radix)
    fo_block = functools.partial(_seqdft_block, tiles_per_batch=seq // TM_TOK, radix=radix)
    h, xn, eidx, cw, rank, cnt = _postmix(of, ob, r, fo, fo_block, x2, p["wo"], p["gg"], p["g2"], p["wr"])

    counts = cnt[:, 0].astype(jnp.int32)
    tiles = (counts + TM_EXP - 1) // TM_EXP
    tile_end = jnp.cumsum(tiles)
    tile_start = tile_end - tiles
    experts = jnp.arange(N_EXPERTS, dtype=jnp.int32)
    seg_row = jnp.sum(jnp.where(eidx[:, :, None] == experts, tile_start * TM_EXP, 0), axis=-1)
    pos = seg_row + rank
    n_tiles = 2 * t // TM_EXP + N_EXPERTS
    tile_ids = jnp.arange(n_tiles, dtype=jnp.int32)
    tile_expert = jnp.minimum(jnp.sum((tile_end[None, :] <= tile_ids[:, None]).astype(jnp.int32), axis=1),
                              N_EXPERTS - 1)
    n_active = tile_end[-1:].astype(jnp.int32)
    tail = n_active + jnp.arange(N_EXPERTS, dtype=jnp.int32)
    fill_tiles = jnp.concatenate([jnp.where(tiles > 0, tile_end - 1, -1),
                                  jnp.where(tail < n_tiles, tail, -1)]).astype(jnp.int32)

    xs = _dispatch(xn, pos, fill_tiles, n_tiles * TM_EXP)
    ys = _experts(xs, tile_expert, n_active, p["w_gate"], p["w_up"], p["w_down"])
    y = _combine(ys, pos, cw.T, h, p["gf"])
    return y.reshape(batch, seq, D_MODEL)


def kernel(x_prompt, x_sample, norm1_g, w_in, w_gk2_f, b_gk_f, w_gk2_b, b_gk_b, gla_norm_g, w_out, norm2_g,
           w_group, w_expert, w_gate, w_up, w_down, norm_f_g):
    p = _prepare_params(norm1_g, w_in, w_gk2_f, b_gk_f, w_gk2_b, b_gk_b, gla_norm_g, w_out,
                        norm2_g, w_group, w_expert, w_gate, w_up, w_down, norm_f_g)
    return (_encoder(x_prompt, p), _encoder(x_sample, p))
```

```python
import functools

import numpy as np
import jax
import jax.numpy as jnp
from jax import lax
from jax.experimental import pallas as pl
from jax.experimental.pallas import tpu as pltpu

D_MODEL = 1024
EPS = 1e-6
GLA_HEADS = 4
GLA_DV = 128
GLA_DK = 64
GLA_WIDTH = GLA_HEADS * GLA_DV
GLA_KEY_WIDTH = GLA_HEADS * GLA_DK
GATE_RANK = 16
GATE_NORMALIZER = 16.0
CHUNK = 64
F_GROUPS = 4
F_GROUP_DIM = 128
F_WIDTH = F_GROUPS * F_GROUP_DIM
N_GROUPS = 4
EXPERTS_PER_GROUP = 8
N_EXPERTS = N_GROUPS * EXPERTS_PER_GROUP
D_EXPERT = 256

LANES = 128
V7X_VMEM_BYTES = 64 * 1024 * 1024
VMEM_LIMIT = 56 * 1024 * 1024

TM_TOK = 512
SUB = 128
DFT_N = 2048
DFT_ROWS = 512
TM_EXP = 512
TM_DISP = 512
TM_COMB = 256
GLA_LOOKAHEAD = 2
ROUTER_ROWS = 48
GATE_COLS = 128
IN_COLS_PAD = 2048 + GATE_COLS
TOK_ROWS = D_MODEL // LANES
N_FILL = 2 * N_EXPERTS

BF16 = jnp.bfloat16
F32 = jnp.float32


def _dot(a, b):
    return jnp.dot(a, b, preferred_element_type=F32)


def _interleave(*stages):
    live = list(stages)
    while live:
        for g in list(live):
            try:
                next(g)
            except StopIteration:
                live.remove(g)


def _dot_nt(a, b):
    return lax.dot_general(a, b, (((1,), (1,)), ((), ())), preferred_element_type=F32)


def _dot_tn(a, b):
    return lax.dot_general(a, b, (((0,), (0,)), ((), ())), preferred_element_type=F32)


def _tri_tables():
    r = np.arange(SUB)
    same = (r[:, None] // CHUNK) == (r[None, :] // CHUNK)
    l_incl = same & (r[None, :] <= r[:, None])
    u_strict = same & (r[None, :] > r[:, None])
    u_incl = same & (r[None, :] >= r[:, None])
    l_strict = same & (r[None, :] < r[:, None])
    fwd = np.concatenate([l_incl, u_strict], 0).astype(np.float32)
    bwd = np.concatenate([u_incl, l_strict], 0).astype(np.float32)
    return fwd, bwd


def _chan_dft_table():
    c = np.arange(F_GROUP_DIM)
    ang = 2.0 * np.pi * ((c[:, None] * c[None, :]) % F_GROUP_DIM) / F_GROUP_DIM
    s = 1.0 / np.sqrt(F_GROUP_DIM)
    return np.concatenate([np.cos(ang) * s, -np.sin(ang) * s], 1).astype(np.float32)


def _seq_dft_tables():
    k = np.arange(DFT_N // 2, dtype=np.int64)
    s = np.arange(DFT_N, dtype=np.int64)
    ang = 2.0 * np.pi * ((k[:, None] * s[None, :]) % DFT_N) / DFT_N
    lower = np.concatenate([np.cos(ang), np.sin(ang)], 1).astype(np.float32)
    edge = np.zeros((16, 2 * DFT_N), np.float32)
    edge[0, :DFT_N] = 1.0 - 2.0 * (s % 2)
    edge[1, :DFT_N] = np.array([1.0, 0.0, -1.0, 0.0])[s % 4]
    edge[1, DFT_N:] = -np.array([0.0, 1.0, 0.0, -1.0])[s % 4]
    i = np.arange(DFT_ROWS)
    mirror = ((i[:, None] >= 1) & (i[None, :] == DFT_ROWS - i[:, None])).astype(np.float32)
    return lower, edge, mirror


def _twiddle_tables(radix):
    k1 = np.arange(radix, dtype=np.int64)[:, None]
    s2 = np.arange(DFT_N, dtype=np.int64)[None, :]
    ang = 2.0 * np.pi * ((k1 * s2) % (radix * DFT_N)) / (radix * DFT_N)
    c = np.repeat(np.cos(ang)[:, :, None], LANES, 2).astype(np.float32)
    s = np.repeat(np.sin(ang)[:, :, None], LANES, 2).astype(np.float32)
    return c, s


_TRI_F, _TRI_B = _tri_tables()
_CHAN_DFT = _chan_dft_table()
_SEQ_DFT_LOWER, _SEQ_DFT_EDGE, _SEQ_DFT_MIRROR = _seq_dft_tables()


def _inproj_project(x_ref, w_ref, v_ref, r_ref, qk_s, fx_s, gt_s):
    x = x_ref[...]
    inv = lax.rsqrt(jnp.mean(x * x, axis=-1, keepdims=True) + EPS)
    xb = x.astype(BF16)
    yield
    qk_s[...] = _dot(xb, w_ref[:, 0:2 * GLA_KEY_WIDTH]) * inv
    yield
    v_ref[...] = (_dot(xb, w_ref[:, 512:1024]) * inv).astype(BF16)
    yield
    r_ref[...] = (_dot(xb, w_ref[:, 1024:1536]) * inv).astype(BF16)
    yield
    fx_s[...] = (_dot(xb, w_ref[:, 1536:2048]) * inv).astype(BF16)
    yield
    gt_s[...] = (_dot(xb, w_ref[:, 2048:IN_COLS_PAD]) * inv).astype(BF16)


def _inproj_decay(qk_s, fx_s, gt_s, wg_ref, bg_ref, trif_ref, trib_ref, cs_ref,
                  qf_ref, kf_ref, tf_ref, qb_ref, kb_ref, tb_ref, u_ref, decf_ref, decb_ref, tot_ref):
    z = _dot(gt_s[...], wg_ref[...]) + bg_ref[...]
    la = (jnp.minimum(z, 0.0) - jnp.log1p(jnp.exp(-jnp.abs(z)))) * (1.0 / GATE_NORMALIZER)
    la_hi = la.astype(BF16)
    la_lo = (la - la_hi.astype(F32)).astype(BF16)
    trif = trif_ref[...]
    trib = trib_ref[...]
    scale = GLA_DK ** -0.5
    for s in range(TM_TOK // SUB):
        yield
        rows = slice(s * SUB, (s + 1) * SUB)
        q = qk_s[rows, 0:GLA_KEY_WIDTH]
        k = qk_s[rows, GLA_KEY_WIDTH:2 * GLA_KEY_WIDTH]
        rf = _dot(trif, la_hi[rows, 0:GLA_KEY_WIDTH]) + _dot(trif, la_lo[rows, 0:GLA_KEY_WIDTH])
        b, tl = rf[0:SUB], rf[SUB:2 * SUB]
        qf_ref[rows, :] = (q * scale * jnp.exp(b)).astype(BF16)
        kf_ref[rows, :] = (k * jnp.exp(-b)).astype(BF16)
        tf_ref[rows, :] = (k * jnp.exp(tl)).astype(BF16)
        totf = b + tl
        tot_ref[0, rows, :] = totf[:, 0:LANES]
        tot_ref[1, rows, :] = totf[:, LANES:]
        rb = _dot(trib, la_hi[rows, GLA_KEY_WIDTH:]) + _dot(trib, la_lo[rows, GLA_KEY_WIDTH:])
        c, tlb = rb[0:SUB], rb[SUB:2 * SUB]
        qb_ref[rows, :] = (q * scale * jnp.exp(c)).astype(BF16)
        kb_ref[rows, :] = (k * jnp.exp(-c)).astype(BF16)
        tb_ref[rows, :] = (k * jnp.exp(tlb)).astype(BF16)
        totb = c + tlb
        tot_ref[2, rows, :] = totb[:, 0:LANES]
        tot_ref[3, rows, :] = totb[:, LANES:]
    yield
    chunk_rows = pl.ds(0, TM_TOK // CHUNK, stride=CHUNK)
    decf_ref[:, 0:LANES] = jnp.exp(tot_ref[0, chunk_rows, :])
    decf_ref[:, LANES:] = jnp.exp(tot_ref[1, chunk_rows, :])
    decb_ref[:, 0:LANES] = jnp.exp(tot_ref[2, chunk_rows, :])
    decb_ref[:, LANES:] = jnp.exp(tot_ref[3, chunk_rows, :])
    cs = cs_ref[...]
    for g in range(F_GROUPS):
        res = _dot(fx_s[:, g * LANES:(g + 1) * LANES], cs)
        u_ref[:, g * LANES:(g + 1) * LANES] = res[:, 0:LANES].astype(BF16)
        u_ref[:, F_WIDTH + g * LANES:F_WIDTH + (g + 1) * LANES] = res[:, LANES:].astype(BF16)


def _inproj_body(x_ref, w_ref, wg_ref, bg_ref, trif_ref, trib_ref, cs_ref,
                 qf_ref, kf_ref, tf_ref, qb_ref, kb_ref, tb_ref, v_ref, r_ref, u_ref, decf_ref, decb_ref,
                 qk0, qk1, fx0, fx1, gt0, gt1, tot_ref):
    i = pl.program_id(0)

    @pl.when(i == 0)
    def _():
        qk1[...] = jnp.zeros_like(qk1)
        fx1[...] = jnp.zeros_like(fx1)
        gt1[...] = jnp.zeros_like(gt1)

    def step(cur, prev):
        stage_a = _inproj_project(x_ref, w_ref, v_ref, r_ref, *cur)
        stage_b = _inproj_decay(*prev, wg_ref, bg_ref, trif_ref, trib_ref, cs_ref,
                                qf_ref, kf_ref, tf_ref, qb_ref, kb_ref, tb_ref, u_ref, decf_ref, decb_ref, tot_ref)
        _interleave(stage_a, stage_b)

    @pl.when(lax.rem(i, 2) == 0)
    def _():
        step((qk0, fx0, gt0), (qk1, fx1, gt1))

    @pl.when(lax.rem(i, 2) == 1)
    def _():
        step((qk1, fx1, gt1), (qk0, fx0, gt0))


def _inproj(x2, w_in_r, wg, bg, trif, trib, cs):
    t = x2.shape[0]
    nt = t // TM_TOK
    cur = lambda i: (jnp.minimum(i, nt - 1), 0)
    prev = lambda i: (jnp.maximum(i - 1, 0), 0)
    full = lambda a: pl.BlockSpec(a.shape, lambda i: (0,) * a.ndim)
    kw = jax.ShapeDtypeStruct((t, GLA_KEY_WIDTH), BF16)
    dec = jax.ShapeDtypeStruct((t // CHUNK, GLA_KEY_WIDTH), F32)
    dec_spec = pl.BlockSpec((TM_TOK // CHUNK, GLA_KEY_WIDTH), prev)
    wide = jax.ShapeDtypeStruct((t, GLA_WIDTH), BF16)
    return pl.pallas_call(
        _inproj_body,
        grid=(nt + 1,),
        in_specs=[pl.BlockSpec((TM_TOK, D_MODEL), cur), full(w_in_r), full(wg), full(bg),
                  full(trif), full(trib), full(cs)],
        out_specs=[pl.BlockSpec((TM_TOK, GLA_KEY_WIDTH), prev)] * 6
                  + [pl.BlockSpec((TM_TOK, GLA_WIDTH), cur), pl.BlockSpec((TM_TOK, GLA_WIDTH), cur),
                     pl.BlockSpec((TM_TOK, 2 * F_WIDTH), prev), dec_spec, dec_spec],
        out_shape=[kw] * 6 + [wide, wide, jax.ShapeDtypeStruct((t, 2 * F_WIDTH), BF16), dec, dec],
        scratch_shapes=[pltpu.VMEM((TM_TOK, 2 * GLA_KEY_WIDTH), F32)] * 2
                       + [pltpu.VMEM((TM_TOK, F_WIDTH), BF16)] * 2
                       + [pltpu.VMEM((TM_TOK, GATE_COLS), BF16)] * 2
                       + [pltpu.VMEM((2 * GLA_KEY_WIDTH // LANES, TM_TOK, LANES), F32)],
        compiler_params=pltpu.CompilerParams(dimension_semantics=("arbitrary",), vmem_limit_bytes=VMEM_LIMIT),
        name="inproj",
    )(x2, w_in_r, wg, bg, trif, trib, cs)


def _gla_local(q_ref, k_ref, t_ref, v_ref, c, p, causal, m_lo, mv_lo, bd):
    rows = slice(c * CHUNK, (c + 1) * CHUNK)
    kl = slice(p * LANES, (p + 1) * LANES)
    vl = slice(p * 2 * GLA_DV, (p + 1) * 2 * GLA_DV)
    qd = q_ref[rows, kl]
    kd = k_ref[rows, kl]
    kt = t_ref[rows, kl]
    vv = v_ref[rows, vl]
    zk = jnp.zeros_like(kd)
    zv = jnp.zeros_like(vv)
    kbd = jnp.concatenate([jnp.where(m_lo, kd, zk), jnp.where(m_lo, zk, kd)], axis=0)
    att = _dot_nt(qd, kbd)
    att = jnp.where(causal, att, 0.0).astype(BF16)
    vbd = jnp.concatenate([jnp.where(mv_lo, vv, zv), jnp.where(mv_lo, zv, vv)], axis=0)
    kv = jnp.where(bd, _dot_tn(vv, kt), 0.0)
    return qd, att, vbd, kv


def _gla_body(qf_ref, kf_ref, tf_ref, vf_ref, df_ref, qb_ref, kb_ref, tb_ref, vb_ref, db_ref,
              of_ref, ob_ref, sf_ref, sb_ref):
    @pl.when(pl.program_id(1) == 0)
    def _():
        sf_ref[...] = jnp.zeros_like(sf_ref)
        sb_ref[...] = jnp.zeros_like(sb_ref)

    lane = lax.broadcasted_iota(jnp.int32, (CHUNK, LANES), 1)
    row = lax.broadcasted_iota(jnp.int32, (CHUNK, LANES), 0)
    m_lo = lane < GLA_DK
    col = lane & (CHUNK - 1)
    causal_f = row >= col
    causal_b = row <= col
    mv_lo = lax.broadcasted_iota(jnp.int32, (CHUNK, 2 * GLA_DV), 1) < GLA_DV
    bd = ((lax.broadcasted_iota(jnp.int32, (2 * GLA_DV, LANES), 0) < GLA_DV)
          == (lax.broadcasted_iota(jnp.int32, (2 * GLA_DV, LANES), 1) < GLA_DK))
    n = TM_TOK // CHUNK
    pairs = range(GLA_HEADS // 2)
    dirs = ((qf_ref, kf_ref, tf_ref, vf_ref, df_ref, of_ref, sf_ref, causal_f, lambda j: j),
            (qb_ref, kb_ref, tb_ref, vb_ref, db_ref, ob_ref, sb_ref, causal_b, lambda j: n - 1 - j))

    def local(j):
        return [[_gla_local(q, k, t, v, order(j), p, causal, m_lo, mv_lo, bd) for p in pairs]
                for (q, k, t, v, _, _, _, causal, order) in dirs]

    state = [[s_ref[p] for p in pairs] for (_, _, _, _, _, _, s_ref, _, _) in dirs]
    ahead = [local(j) for j in range(GLA_LOOKAHEAD)]
    for j in range(n):
        cur = ahead.pop(0)
        if j + GLA_LOOKAHEAD < n:
            ahead.append(local(j + GLA_LOOKAHEAD))
        for d, (_, _, _, _, d_ref, o_ref, _, _, order) in enumerate(dirs):
            c = order(j)
            dec = d_ref[c:c + 1, :]
            for p in pairs:
                qd, att, vbd, kv = cur[d][p]
                st = state[d][p]
                o = _dot(att, vbd) + _dot_nt(qd, st.astype(BF16))
                o_ref[c * CHUNK:(c + 1) * CHUNK, p * 2 * GLA_DV:(p + 1) * 2 * GLA_DV] = o.astype(o_ref.dtype)
                state[d][p] = st * dec[:, p * LANES:(p + 1) * LANES] + kv
    for d, (_, _, _, _, _, _, s_ref, _, _) in enumerate(dirs):
        for p in pairs:
            s_ref[p] = state[d][p]


def _gla(qf, kf, tf, qb, kb, tb, v, decf, decb, batch, seq):
    t = batch * seq
    nt = seq // TM_TOK
    fwd = lambda b, i: (b * nt + i, 0)
    bwd = lambda b, i: (b * nt + nt - 1 - i, 0)
    ks = lambda m: pl.BlockSpec((TM_TOK, GLA_KEY_WIDTH), m)
    vs = lambda m: pl.BlockSpec((TM_TOK, GLA_WIDTH), m)
    ds = lambda m: pl.BlockSpec((TM_TOK // CHUNK, GLA_KEY_WIDTH), m)
    o = jax.ShapeDtypeStruct((t, GLA_WIDTH), BF16)
    state = pltpu.VMEM((GLA_HEADS // 2, 2 * GLA_DV, LANES), F32)
    return pl.pallas_call(
        _gla_body,
        grid=(batch, nt),
        in_specs=[ks(fwd), ks(fwd), ks(fwd), vs(fwd), ds(fwd), ks(bwd), ks(bwd), ks(bwd), vs(bwd), ds(bwd)],
        out_specs=[vs(fwd), vs(bwd)],
        out_shape=[o, o],
        scratch_shapes=[state, state],
        compiler_params=pltpu.CompilerParams(dimension_semantics=("arbitrary", "arbitrary"),
                                             vmem_limit_bytes=VMEM_LIMIT),
        name="gla",
    )(qf, kf, tf, v, decf, qb, kb, tb, v, decb)


RADIX_ROWS = 256


def _radix4_body(z_ref, twc_ref, tws_ref, y_ref):
    z = [z_ref[s].astype(F32) for s in range(4)]
    re = [a[:, 0:F_WIDTH] for a in z]
    im = [a[:, F_WIDTH:] for a in z]
    ar, ai = re[0] + re[2], im[0] + im[2]
    br, bi = re[0] - re[2], im[0] - im[2]
    cr, ci = re[1] + re[3], im[1] + im[3]
    dr, di = re[1] - re[3], im[1] - im[3]
    y = [(ar + cr, ai + ci), (br + di, bi - dr), (ar - cr, ai - ci), (br - di, bi + dr)]
    y_ref[0, :, 0:F_WIDTH] = y[0][0].astype(BF16)
    y_ref[0, :, F_WIDTH:] = y[0][1].astype(BF16)
    for k1 in range(1, 4):
        c = jnp.concatenate([twc_ref[k1]] * (F_WIDTH // LANES), axis=1)
        s = jnp.concatenate([tws_ref[k1]] * (F_WIDTH // LANES), axis=1)
        yr, yi = y[k1]
        y_ref[k1, :, 0:F_WIDTH] = (yr * c + yi * s).astype(BF16)
        y_ref[k1, :, F_WIDTH:] = (yi * c - yr * s).astype(BF16)


def _radix4(u, batch):
    z = u.reshape(batch, 4, DFT_N, 2 * F_WIDTH)
    twc, tws = _twiddle_tables(4)
    twc, tws = jnp.asarray(twc), jnp.asarray(tws)
    nr = DFT_N // RADIX_ROWS
    blk = pl.BlockSpec((None, 4, RADIX_ROWS, 2 * F_WIDTH), lambda b, i: (b, 0, i, 0))
    tw = pl.BlockSpec((4, RADIX_ROWS, LANES), lambda b, i: (0, i, 0))
    y = pl.pallas_call(
        _radix4_body,
        grid=(batch, nr),
        in_specs=[blk, tw, tw],
        out_specs=blk,
        out_shape=jax.ShapeDtypeStruct((batch, 4, DFT_N, 2 * F_WIDTH), BF16),
        compiler_params=pltpu.CompilerParams(dimension_semantics=("arbitrary", "arbitrary"),
                                             vmem_limit_bytes=VMEM_LIMIT),
        name="radix4",
    )(z, twc, tws)
    return y


def _seqdft_body(cs_ref, edge_ref, mir_ref, ab_ref, o_ref, *, radix, scale):
    j = pl.program_id(1)
    first_row = lax.broadcasted_iota(jnp.int32, (DFT_ROWS, F_WIDTH), 0) == 0
    half = radix * DFT_ROWS
    rows = pl.ds(pl.multiple_of(j * DFT_ROWS, DFT_ROWS), DFT_ROWS)
    for k1 in range(radix):
        a = ab_ref[k1, :, 0:F_WIDTH]
        b = ab_ref[k1, :, F_WIDTH:]
        p = _dot(cs_ref[rows, 0:DFT_N], a)
        q = _dot(cs_ref[rows, DFT_N:], b)
        direct = (p + q) * scale
        edge = (_dot(edge_ref[:, 0:DFT_N], a) + _dot(edge_ref[:, DFT_N:], b)) * scale
        mirror = _dot(mir_ref[...], ((p - q) * scale).astype(BF16))
        mirror = jnp.where(first_row, jnp.where(j == 0, edge[1:2], edge[0:1]), mirror)
        for c in range(F_WIDTH // LANES):
            cols = slice(c * LANES, (c + 1) * LANES)
            o_ref[c, pl.ds(k1, DFT_ROWS, stride=radix), :] = direct[:, cols]
            o_ref[c, pl.ds(half + k1, DFT_ROWS, stride=radix), :] = mirror[:, cols]


def _seqdft_block(token_tile, tiles_per_batch, radix):
    b = token_tile // tiles_per_batch
    it = token_tile % tiles_per_batch
    ft = it // radix
    j = jnp.where(ft < 2, ft, 3 - ft)
    return ((b * 2 + j) * 2 + ft // 2) * radix + it % radix


def _seqdft(ab, cs, edge, mir, batch, radix):
    scale = float(1.0 / np.sqrt(radix * DFT_N))
    nj = DFT_N // (2 * DFT_ROWS)
    full = lambda a: pl.BlockSpec(a.shape, lambda b, j: (0,) * a.ndim)
    return pl.pallas_call(
        functools.partial(_seqdft_body, radix=radix, scale=scale),
        grid=(batch, nj),
        in_specs=[pl.BlockSpec(cs.shape, lambda b, j: (0, 0), pipeline_mode=pl.Buffered(1)), full(edge), full(mir),
                  pl.BlockSpec((None, radix, DFT_N, 2 * F_WIDTH), lambda b, j: (b, 0, 0, 0),
                               pipeline_mode=pl.Buffered(1))],
        out_specs=pl.BlockSpec((F_WIDTH // LANES, 2 * radix * DFT_ROWS, LANES), lambda b, j: (0, b * nj + j, 0)),
        out_shape=jax.ShapeDtypeStruct((F_WIDTH // LANES, batch * radix * DFT_N, LANES), F32),
        compiler_params=pltpu.CompilerParams(dimension_semantics=("arbitrary", "arbitrary"),
                                             vmem_limit_bytes=VMEM_LIMIT),
        name="seqdft",
    )(cs, edge, mir, ab)


def _first_index(hit, rows):
    return jnp.min(jnp.where(hit, rows.astype(F32), 1e6), axis=0, keepdims=True).astype(jnp.int32)


def _postmix_body(of_ref, ob_ref, r_ref, fo_ref, x_ref, wo_ref, gg_ref, g2_ref, wr_ref,
                  h_ref, xn_ref, eidx_ref, cw_ref, rank_ref, cnt_ref, carry_ref):
    @pl.when(pl.program_id(0) == 0)
    def _():
        carry_ref[...] = jnp.zeros_like(carry_ref)

    o = of_ref[...].astype(F32) + ob_ref[...].astype(F32)
    r = r_ref[...].astype(F32)
    parts = []
    for hd in range(GLA_HEADS):
        sl = slice(hd * GLA_DV, (hd + 1) * GLA_DV)
        oh = o[:, sl]
        oh = oh * lax.rsqrt(jnp.mean(oh * oh, axis=-1, keepdims=True) + EPS)
        rh = r[:, sl]
        parts.append((oh * gg_ref[...] * (rh * jax.nn.sigmoid(rh))).astype(BF16))
    on = jnp.concatenate(parts, axis=1)
    fo = jnp.concatenate([fo_ref[c] for c in range(F_WIDTH // LANES)], axis=1)
    mixed = _dot(on, wo_ref[0:GLA_WIDTH, :]) + _dot(fo.astype(BF16), wo_ref[GLA_WIDTH:, :])
    h = x_ref[...] + mixed
    h_ref[...] = h
    xn = h * lax.rsqrt(jnp.mean(h * h, axis=-1, keepdims=True) + EPS) * g2_ref[...]
    _store_token_tiles(xn_ref, xn)
    logits = _dot_nt(wr_ref[...], xn.astype(BF16))

    sub8 = lax.broadcasted_iota(jnp.int32, (8, TM_TOK), 0)
    lg = jnp.where(sub8 < N_GROUPS, logits[0:8], -jnp.inf)
    gmax = jnp.max(lg, axis=0, keepdims=True)
    g_w = 1.0 / jnp.sum(jnp.exp(lg - gmax), axis=0, keepdims=True)
    g_sel = _first_index(lg == gmax, sub8)
    sel = logits[8:16]
    for g in range(1, N_GROUPS):
        sel = jnp.where(g_sel == g, logits[8 + 8 * g:16 + 8 * g], sel)
    m1 = jnp.max(sel, axis=0, keepdims=True)
    i1 = _first_index(sel == m1, sub8)
    sel2 = jnp.where(sub8 == i1, -jnp.inf, sel)
    m2 = jnp.max(sel2, axis=0, keepdims=True)
    i2 = _first_index(sel2 == m2, sub8)
    e21 = jnp.exp(m2 - m1)
    w1 = 1.0 / (1.0 + e21)
    w2 = e21 / (1.0 + e21)
    e1 = g_sel * EXPERTS_PER_GROUP + i1
    e2 = g_sel * EXPERTS_PER_GROUP + i2
    eidx_ref[...] = jnp.concatenate([e1, e2], axis=0)
    cw_ref[...] = jnp.concatenate([g_w * w1, g_w * w2], axis=0)

    sub = lax.broadcasted_iota(jnp.int32, (N_EXPERTS, TM_TOK), 0)
    oh1 = sub == e1
    oh2 = sub == e2
    oh1b = jnp.where(oh1, 1.0, 0.0).astype(BF16)
    oh2b = jnp.where(oh2, 1.0, 0.0).astype(BF16)
    before = (lax.broadcasted_iota(jnp.int32, (TM_TOK, TM_TOK), 0)
              < lax.broadcasted_iota(jnp.int32, (TM_TOK, TM_TOK), 1))
    before = jnp.where(before, 1.0, 0.0).astype(BF16)
    ones = jnp.ones((TM_TOK, LANES), BF16)
    oh12 = jnp.concatenate([oh1b, oh2b], axis=0)
    p12 = _dot(oh12, before)
    c12 = _dot(oh12, ones)
    p1, p2 = p12[0:N_EXPERTS], p12[N_EXPERTS:]
    c1, c2 = c12[0:N_EXPERTS], c12[N_EXPERTS:]
    carry = carry_ref[...]
    rep = TM_TOK // LANES
    base1 = jnp.concatenate([carry] * rep, axis=1)
    base2 = jnp.concatenate([carry + c1] * rep, axis=1)
    rk1 = jnp.sum(jnp.where(oh1, p1 + base1, 0.0), axis=0, keepdims=True)
    rk2 = jnp.sum(jnp.where(oh2, p2 + base2, 0.0), axis=0, keepdims=True)
    rank_ref[...] = jnp.concatenate([rk1, rk2], axis=0).astype(jnp.int32)
    carry = carry + c1 + c2
    carry_ref[...] = carry
    cnt_ref[...] = carry


def _postmix(of, ob, r, fo, fo_block, x2, wo, gg, g2, wr):
    t = x2.shape[0]
    nt = t // TM_TOK
    tok = lambda w: pl.BlockSpec((TM_TOK, w), lambda i: (i, 0))
    full = lambda a: pl.BlockSpec(a.shape, lambda i: (0,) * a.ndim)
    lane2 = pl.BlockSpec((2, TM_TOK), lambda i: (0, i))
    return pl.pallas_call(
        _postmix_body,
        grid=(nt,),
        in_specs=[tok(GLA_WIDTH), tok(GLA_WIDTH), tok(GLA_WIDTH),
                  pl.BlockSpec((F_WIDTH // LANES, TM_TOK, LANES), lambda i: (0, fo_block(i), 0)), tok(D_MODEL),
                  full(wo), full(gg), full(g2), full(wr)],
        out_specs=[tok(D_MODEL), pl.BlockSpec((TM_TOK * TOK_ROWS, LANES), lambda i: (i, 0)), lane2, lane2, lane2,
                   pl.BlockSpec((N_EXPERTS, LANES), lambda i: (0, 0))],
        out_shape=[jax.ShapeDtypeStruct((t, D_MODEL), F32), jax.ShapeDtypeStruct((t * TOK_ROWS, LANES), F32),
                   jax.ShapeDtypeStruct((2, t), jnp.int32), jax.ShapeDtypeStruct((2, t), F32),
                   jax.ShapeDtypeStruct((2, t), jnp.int32), jax.ShapeDtypeStruct((N_EXPERTS, LANES), F32)],
        scratch_shapes=[pltpu.VMEM((N_EXPERTS, LANES), F32)],
        compiler_params=pltpu.CompilerParams(dimension_semantics=("arbitrary",), vmem_limit_bytes=VMEM_LIMIT),
        name="postmix",
    )(of, ob, r, fo, x2, wo, gg, g2, wr)


def _store_token_tiles(ref, val):
    n = val.shape[0]
    for c in range(TOK_ROWS):
        ref[pl.ds(c, n, stride=TOK_ROWS), :] = val[:, c * LANES:(c + 1) * LANES]


def _load_token_tiles(ref, n):
    return jnp.concatenate([ref[pl.ds(c, n, stride=TOK_ROWS), :] for c in range(TOK_ROWS)], axis=1)


def _token_rows(tok):
    if isinstance(tok, int):
        return pl.ds(tok * TOK_ROWS, TOK_ROWS)
    return pl.ds(pl.multiple_of(tok * TOK_ROWS, TOK_ROWS), TOK_ROWS)


def _row_copy(src_hbm, dst_hbm, src_tok, dst_tok, sem):
    return pltpu.make_async_copy(src_hbm.at[_token_rows(src_tok)], dst_hbm.at[_token_rows(dst_tok)], sem)


def _dispatch_body(fill_ref, pos_ref, xn_ref, xs_hbm, zeros_ref, fill_sem, row_sem):
    i = pl.program_id(0)

    @pl.when(i == 0)
    def _():
        zeros_ref[...] = jnp.zeros_like(zeros_ref)
        tile_rows = TM_EXP * TOK_ROWS

        def fill(e):
            rows = pl.ds(pl.multiple_of(fill_ref[e] * tile_rows, tile_rows), tile_rows)
            return pltpu.make_async_copy(zeros_ref, xs_hbm.at[rows], fill_sem)

        for e in range(N_FILL):
            @pl.when(fill_ref[e] >= 0)
            def _():
                fill(e).start()
        for e in range(N_FILL):
            @pl.when(fill_ref[e] >= 0)
            def _():
                fill(e).wait()

    for j in range(TM_DISP):
        for s in range(2):
            _row_copy(xn_ref, xs_hbm, j, pos_ref[s, j], row_sem).start(priority=s)
    for j in range(TM_DISP):
        for s in range(2):
            _row_copy(xn_ref, xs_hbm, j, 0, row_sem).wait()


def _dispatch(xn, pos, fill_rows, rows_total):
    t = xn.shape[0] // TOK_ROWS
    grid_spec = pltpu.PrefetchScalarGridSpec(
        num_scalar_prefetch=1,
        grid=(t // TM_DISP,),
        in_specs=[pl.BlockSpec((2, TM_DISP), lambda i, fill: (0, i), memory_space=pltpu.SMEM),
                  pl.BlockSpec((TM_DISP * TOK_ROWS, LANES), lambda i, fill: (i, 0))],
        out_specs=pl.BlockSpec(memory_space=pl.ANY),
        scratch_shapes=[pltpu.VMEM((TM_EXP * TOK_ROWS, LANES), F32), pltpu.SemaphoreType.DMA(()),
                        pltpu.SemaphoreType.DMA(())],
    )
    return pl.pallas_call(
        _dispatch_body,
        grid_spec=grid_spec,
        out_shape=jax.ShapeDtypeStruct((rows_total * TOK_ROWS, LANES), F32),
        compiler_params=pltpu.CompilerParams(dimension_semantics=("arbitrary",), vmem_limit_bytes=VMEM_LIMIT),
        name="dispatch",
    )(fill_rows, pos, xn)


def _experts_body(te_ref, na_ref, xs_ref, wg_ref, wu_ref, wd_ref, ys_ref, wg_s, wu_s, wd_s):
    i = pl.program_id(0)
    active = i < na_ref[0]

    @pl.when(active & ((i == 0) | (te_ref[i] != te_ref[jnp.maximum(i - 1, 0)])))
    def _():
        wg_s[...] = wg_ref[...].astype(BF16)
        wu_s[...] = wu_ref[...].astype(BF16)
        wd_s[...] = wd_ref[...].astype(BF16)

    @pl.when(active)
    def _():
        x = _load_token_tiles(xs_ref, TM_EXP).astype(BF16)
        gate = _dot(x, wg_s[...])
        up = _dot(x, wu_s[...])
        hid = (gate * jax.nn.sigmoid(gate) * up).astype(BF16)
        _store_token_tiles(ys_ref, _dot(hid, wd_s[...]))

    @pl.when(jnp.logical_not(active))
    def _():
        ys_ref[...] = jnp.zeros_like(ys_ref)


def _experts(xs, tile_expert, n_active, wg, wu, wd):
    n_tiles = xs.shape[0] // (TM_EXP * TOK_ROWS)
    row = lambda i, te, na: (i, 0)
    row_in = lambda i, te, na: (jnp.minimum(i, na[0] - 1), 0)
    wsel = lambda i, te, na: (te[jnp.minimum(i, na[0] - 1)], 0, 0)
    grid_spec = pltpu.PrefetchScalarGridSpec(
        num_scalar_prefetch=2,
        grid=(n_tiles,),
        in_specs=[pl.BlockSpec((TM_EXP * TOK_ROWS, LANES), row_in),
                  pl.BlockSpec((None, D_MODEL, D_EXPERT), wsel),
                  pl.BlockSpec((None, D_MODEL, D_EXPERT), wsel),
                  pl.BlockSpec((None, D_EXPERT, D_MODEL), wsel)],
        out_specs=pl.BlockSpec((TM_EXP * TOK_ROWS, LANES), row),
        scratch_shapes=[pltpu.VMEM((D_MODEL, D_EXPERT), BF16), pltpu.VMEM((D_MODEL, D_EXPERT), BF16),
                        pltpu.VMEM((D_EXPERT, D_MODEL), BF16)],
    )
    return pl.pallas_call(
        _experts_body,
        grid_spec=grid_spec,
        out_shape=jax.ShapeDtypeStruct(xs.shape, F32),
        compiler_params=pltpu.CompilerParams(dimension_semantics=("arbitrary",), vmem_limit_bytes=VMEM_LIMIT),
        name="experts",
    )(tile_expert, n_active, xs, wg, wu, wd)


def _combine_body(pos_ref, nxt_ref, ys_hbm, cw_ref, h_ref, gf_ref, y_ref, buf_ref, sem):
    i = pl.program_id(0)
    slot = lax.rem(i, 2)

    def row_gather(p_ref, sl, s, j):
        return pltpu.make_async_copy(ys_hbm.at[_token_rows(p_ref[s, j])], buf_ref.at[sl, s, _token_rows(j)],
                                     sem.at[sl])

    def issue_tile(p_ref, sl):
        for j in range(TM_COMB):
            for s in range(2):
                row_gather(p_ref, sl, s, j).start(priority=s)

    @pl.when(i == 0)
    def _():
        issue_tile(pos_ref, slot)

    @pl.when(i + 1 < pl.num_programs(0))
    def _():
        issue_tile(nxt_ref, 1 - slot)

    for j in range(TM_COMB):
        for s in range(2):
            pltpu.make_async_copy(ys_hbm.at[_token_rows(0)], buf_ref.at[slot, s, _token_rows(j)], sem.at[slot]).wait()

    cw = cw_ref[...]
    y0 = _load_token_tiles(buf_ref.at[slot, 0], TM_COMB)
    y1 = _load_token_tiles(buf_ref.at[slot, 1], TM_COMB)
    h = h_ref[...] + cw[:, 0:1] * y0 + cw[:, 1:2] * y1
    y_ref[...] = h * lax.rsqrt(jnp.mean(h * h, axis=-1, keepdims=True) + EPS) * gf_ref[...]


def _combine(ys, pos, cw_t, h, gf):
    t = h.shape[0]
    n = t // TM_COMB
    tok = pl.BlockSpec((TM_COMB, D_MODEL), lambda i: (i, 0))
    return pl.pallas_call(
        _combine_body,
        grid=(n,),
        in_specs=[pl.BlockSpec((2, TM_COMB), lambda i: (0, i), memory_space=pltpu.SMEM),
                  pl.BlockSpec((2, TM_COMB), lambda i: (0, jnp.minimum(i + 1, n - 1)), memory_space=pltpu.SMEM),
                  pl.BlockSpec(memory_space=pl.ANY),
                  pl.BlockSpec((TM_COMB, 2), lambda i: (i, 0)),
                  tok,
                  pl.BlockSpec((1, D_MODEL), lambda i: (0, 0))],
        out_specs=tok,
        out_shape=jax.ShapeDtypeStruct((t, D_MODEL), F32),
        scratch_shapes=[pltpu.VMEM((2, 2, TM_COMB * TOK_ROWS, LANES), F32), pltpu.SemaphoreType.DMA((2,))],
        compiler_params=pltpu.CompilerParams(dimension_semantics=("arbitrary",), vmem_limit_bytes=VMEM_LIMIT),
        name="combine",
    )(pos, pos, ys, cw_t, h, gf)


def _prepare_params(norm1_g, w_in, w_gk2_f, b_gk_f, w_gk2_b, b_gk_b, gla_norm_g, w_out,
                    norm2_g, w_group, w_expert, w_gate, w_up, w_down, norm_f_g):
    w = w_in[0] * norm1_g[0][:, None]
    gate_lo = 2 * GLA_KEY_WIDTH + 2 * GLA_WIDTH
    gate_hi = gate_lo + 2 * GATE_RANK
    w_in_r = jnp.concatenate([w[:, :gate_lo], w[:, gate_hi:], w[:, gate_lo:gate_hi],
                              jnp.zeros((D_MODEL, GATE_COLS - 2 * GATE_RANK), F32)], axis=1).astype(BF16)
    zk = jnp.zeros((GATE_RANK, GLA_KEY_WIDTH), F32)
    wg = jnp.concatenate([jnp.concatenate([w_gk2_f[0], zk], axis=1), jnp.concatenate([zk, w_gk2_b[0]], axis=1),
                          jnp.zeros((GATE_COLS - 2 * GATE_RANK, 2 * GLA_KEY_WIDTH), F32)], axis=0).astype(BF16)
    bg = jnp.concatenate([b_gk_f[0], b_gk_b[0]])[None, :]
    wr = jnp.concatenate([w_group[0].T, jnp.zeros((8 - N_GROUPS, D_MODEL), F32), w_expert[0].T,
                          jnp.zeros((ROUTER_ROWS - 8 - N_EXPERTS, D_MODEL), F32)], axis=0).astype(BF16)
    return dict(
        w_in_r=w_in_r, wg=wg, bg=bg,
        gg=gla_norm_g[0][None, :], wo=w_out[0].astype(BF16), g2=norm2_g[0][None, :], wr=wr,
        w_gate=w_gate[0], w_up=w_up[0], w_down=w_down[0],
        gf=norm_f_g[None, :],
        trif=jnp.asarray(_TRI_F).astype(BF16), trib=jnp.asarray(_TRI_B).astype(BF16),
        chan_dft=jnp.asarray(_CHAN_DFT).astype(BF16), seq_dft=jnp.asarray(_SEQ_DFT_LOWER).astype(BF16),
        seq_edge=jnp.asarray(_SEQ_DFT_EDGE).astype(BF16), seq_mirror=jnp.asarray(_SEQ_DFT_MIRROR).astype(BF16))


def _encoder(x, p):
    batch, seq, _ = x.shape
    t = batch * seq
    assert seq % DFT_N == 0 and seq // DFT_N in (1, 4), "sequence DFT supports seq = 2048 or 8192"
    radix = seq // DFT_N
    x2 = x.reshape(t, D_MODEL)
    qf, kf, tf, qb, kb, tb, v, r, u, decf, decb = _inproj(x2, p["w_in_r"], p["wg"], p["bg"],
                                                          p["trif"], p["trib"], p["chan_dft"])
    of, ob = _gla(qf, kf, tf, qb, kb, tb, v, decf, decb, batch, seq)
    if radix == 1:
        ab = u.reshape(batch, 1, DFT_N, 2 * F_WIDTH)
    else:
        ab = _radix4(u, batch)
    fo = _seqdft(ab, p["seq_dft"], p["seq_edge"], p["seq_mirror"], batch, radix)
    fo_block = functools.partial(_seqdft_block, tiles_per_batch=seq // TM_TOK, radix=radix)
    h, xn, eidx, cw, rank, cnt = _postmix(of, ob, r, fo, fo_block, x2, p["wo"], p["gg"], p["g2"], p["wr"])

    counts = cnt[:, 0].astype(jnp.int32)
    tiles = (counts + TM_EXP - 1) // TM_EXP
    tile_end = jnp.cumsum(tiles)
    tile_start = tile_end - tiles
    experts = jnp.arange(N_EXPERTS, dtype=jnp.int32)
    seg_row = jnp.sum(jnp.where(eidx[:, :, None] == experts, tile_start * TM_EXP, 0), axis=-1)
    pos = seg_row + rank
    n_tiles = 2 * t // TM_EXP + N_EXPERTS
    tile_ids = jnp.arange(n_tiles, dtype=jnp.int32)
    tile_expert = jnp.minimum(jnp.sum((tile_end[None, :] <= tile_ids[:, None]).astype(jnp.int32), axis=1),
                              N_EXPERTS - 1)
    n_active = tile_end[-1:].astype(jnp.int32)
    tail = n_active + jnp.arange(N_EXPERTS, dtype=jnp.int32)
    fill_tiles = jnp.concatenate([jnp.where(tiles > 0, tile_end - 1, -1),
                                  jnp.where(tail < n_tiles, tail, -1)]).astype(jnp.int32)

    xs = _dispatch(xn, pos, fill_tiles, n_tiles * TM_EXP)
    ys = _experts(xs, tile_expert, n_active, p["w_gate"], p["w_up"], p["w_down"])
    y = _combine(ys, pos, cw.T, h, p["gf"])
    return y.reshape(batch, seq, D_MODEL)


def kernel(x_prompt, x_sample, norm1_g, w_in, w_gk2_f, b_gk_f, w_gk2_b, b_gk_b, gla_norm_g, w_out, norm2_g,
           w_group, w_expert, w_gate, w_up, w_down, norm_f_g):
    p = _prepare_params(norm1_g, w_in, w_gk2_f, b_gk_f, w_gk2_b, b_gk_b, gla_norm_g, w_out,
                        norm2_g, w_group, w_expert, w_gate, w_up, w_down, norm_f_g)
    return (_encoder(x_prompt, p), _encoder(x_sample, p))
```

```python
import functools

import numpy as np
import jax
import jax.numpy as jnp
from jax import lax
from jax.experimental import pallas as pl
from jax.experimental.pallas import tpu as pltpu

D_MODEL = 1024
EPS = 1e-6
GLA_HEADS = 4
GLA_DV = 128
GLA_DK = 64
GLA_WIDTH = GLA_HEADS * GLA_DV
GLA_KEY_WIDTH = GLA_HEADS * GLA_DK
GATE_RANK = 16
GATE_NORMALIZER = 16.0
CHUNK = 64
F_GROUPS = 4
F_GROUP_DIM = 128
F_WIDTH = F_GROUPS * F_GROUP_DIM
N_GROUPS = 4
EXPERTS_PER_GROUP = 8
N_EXPERTS = N_GROUPS * EXPERTS_PER_GROUP
D_EXPERT = 256

LANES = 128
V7X_VMEM_BYTES = 64 * 1024 * 1024
VMEM_LIMIT = 56 * 1024 * 1024

TM_TOK = 512
SUB = 128
DFT_N = 2048
DFT_ROWS = 512
DFT_TEMP_BYTES = 5 * 512 * 512 * 4
TM_EXP = 512
TM_DISP = 512
TM_COMB = 256
GLA_LOOKAHEAD = 2
ROUTER_ROWS = 48
GATE_COLS = 128
IN_COLS_PAD = 2048 + GATE_COLS
TOK_ROWS = D_MODEL // LANES
N_FILL = 2 * N_EXPERTS

BF16 = jnp.bfloat16
F32 = jnp.float32


def _dot(a, b):
    return jnp.dot(a, b, preferred_element_type=F32)


def _interleave(*stages):
    live = list(stages)
    while live:
        for g in list(live):
            try:
                next(g)
            except StopIteration:
                live.remove(g)


def _dot_nt(a, b):
    return lax.dot_general(a, b, (((1,), (1,)), ((), ())), preferred_element_type=F32)


def _dot_tn(a, b):
    return lax.dot_general(a, b, (((0,), (0,)), ((), ())), preferred_element_type=F32)


def _tri_tables():
    r = np.arange(SUB)
    same = (r[:, None] // CHUNK) == (r[None, :] // CHUNK)
    l_incl = same & (r[None, :] <= r[:, None])
    u_strict = same & (r[None, :] > r[:, None])
    u_incl = same & (r[None, :] >= r[:, None])
    l_strict = same & (r[None, :] < r[:, None])
    fwd = np.concatenate([l_incl, u_strict], 0).astype(np.float32)
    bwd = np.concatenate([u_incl, l_strict], 0).astype(np.float32)
    return fwd, bwd


def _chan_dft_table():
    c = np.arange(F_GROUP_DIM)
    ang = 2.0 * np.pi * ((c[:, None] * c[None, :]) % F_GROUP_DIM) / F_GROUP_DIM
    s = 1.0 / np.sqrt(F_GROUP_DIM)
    return np.concatenate([np.cos(ang) * s, -np.sin(ang) * s], 1).astype(np.float32)


def _seq_dft_tables():
    k = np.arange(DFT_N // 2, dtype=np.int64)
    s = np.arange(DFT_N, dtype=np.int64)
    ang = 2.0 * np.pi * ((k[:, None] * s[None, :]) % DFT_N) / DFT_N
    lower = np.concatenate([np.cos(ang), np.sin(ang)], 1).astype(np.float32)
    edge = np.zeros((16, 2 * DFT_N), np.float32)
    edge[0, :DFT_N] = 1.0 - 2.0 * (s % 2)
    edge[1, :DFT_N] = np.array([1.0, 0.0, -1.0, 0.0])[s % 4]
    edge[1, DFT_N:] = -np.array([0.0, 1.0, 0.0, -1.0])[s % 4]
    i = np.arange(DFT_ROWS)
    mirror = ((i[:, None] >= 1) & (i[None, :] == DFT_ROWS - i[:, None])).astype(np.float32)
    return lower, edge, mirror


def _twiddle_tables(radix):
    k1 = np.arange(radix, dtype=np.int64)[:, None]
    s2 = np.arange(DFT_N, dtype=np.int64)[None, :]
    ang = 2.0 * np.pi * ((k1 * s2) % (radix * DFT_N)) / (radix * DFT_N)
    c = np.repeat(np.cos(ang)[:, :, None], LANES, 2).astype(np.float32)
    s = np.repeat(np.sin(ang)[:, :, None], LANES, 2).astype(np.float32)
    return c, s


_TRI_F, _TRI_B = _tri_tables()
_CHAN_DFT = _chan_dft_table()
_SEQ_DFT_LOWER, _SEQ_DFT_EDGE, _SEQ_DFT_MIRROR = _seq_dft_tables()


def _inproj_project(x_ref, w_ref, v_ref, r_ref, qk_s, fx_s, gt_s):
    x = x_ref[...]
    inv = lax.rsqrt(jnp.mean(x * x, axis=-1, keepdims=True) + EPS)
    xb = x.astype(BF16)
    yield
    qk_s[...] = _dot(xb, w_ref[:, 0:2 * GLA_KEY_WIDTH]) * inv
    yield
    v_ref[...] = (_dot(xb, w_ref[:, 512:1024]) * inv).astype(BF16)
    yield
    r_ref[...] = (_dot(xb, w_ref[:, 1024:1536]) * inv).astype(BF16)
    yield
    fx_s[...] = (_dot(xb, w_ref[:, 1536:2048]) * inv).astype(BF16)
    yield
    gt_s[...] = (_dot(xb, w_ref[:, 2048:IN_COLS_PAD]) * inv).astype(BF16)


def _inproj_decay(qk_s, fx_s, gt_s, wg_ref, bg_ref, trif_ref, trib_ref, cs_ref,
                  qf_ref, kf_ref, tf_ref, qb_ref, kb_ref, tb_ref, u_ref, decf_ref, decb_ref, tot_ref):
    z = _dot(gt_s[...], wg_ref[...]) + bg_ref[...]
    la = (jnp.minimum(z, 0.0) - jnp.log1p(jnp.exp(-jnp.abs(z)))) * (1.0 / GATE_NORMALIZER)
    la_hi = la.astype(BF16)
    la_lo = (la - la_hi.astype(F32)).astype(BF16)
    trif = trif_ref[...]
    trib = trib_ref[...]
    scale = GLA_DK ** -0.5
    for s in range(TM_TOK // SUB):
        yield
        rows = slice(s * SUB, (s + 1) * SUB)
        q = qk_s[rows, 0:GLA_KEY_WIDTH]
        k = qk_s[rows, GLA_KEY_WIDTH:2 * GLA_KEY_WIDTH]
        rf = _dot(trif, la_hi[rows, 0:GLA_KEY_WIDTH]) + _dot(trif, la_lo[rows, 0:GLA_KEY_WIDTH])
        b, tl = rf[0:SUB], rf[SUB:2 * SUB]
        qf_ref[rows, :] = (q * scale * jnp.exp(b)).astype(BF16)
        kf_ref[rows, :] = (k * jnp.exp(-b)).astype(BF16)
        tf_ref[rows, :] = (k * jnp.exp(tl)).astype(BF16)
        totf = b + tl
        tot_ref[0, rows, :] = totf[:, 0:LANES]
        tot_ref[1, rows, :] = totf[:, LANES:]
        rb = _dot(trib, la_hi[rows, GLA_KEY_WIDTH:]) + _dot(trib, la_lo[rows, GLA_KEY_WIDTH:])
        c, tlb = rb[0:SUB], rb[SUB:2 * SUB]
        qb_ref[rows, :] = (q * scale * jnp.exp(c)).astype(BF16)
        kb_ref[rows, :] = (k * jnp.exp(-c)).astype(BF16)
        tb_ref[rows, :] = (k * jnp.exp(tlb)).astype(BF16)
        totb = c + tlb
        tot_ref[2, rows, :] = totb[:, 0:LANES]
        tot_ref[3, rows, :] = totb[:, LANES:]
    yield
    chunk_rows = pl.ds(0, TM_TOK // CHUNK, stride=CHUNK)
    decf_ref[:, 0:LANES] = jnp.exp(tot_ref[0, chunk_rows, :])
    decf_ref[:, LANES:] = jnp.exp(tot_ref[1, chunk_rows, :])
    decb_ref[:, 0:LANES] = jnp.exp(tot_ref[2, chunk_rows, :])
    decb_ref[:, LANES:] = jnp.exp(tot_ref[3, chunk_rows, :])
    cs = cs_ref[...]
    for g in range(F_GROUPS):
        res = _dot(fx_s[:, g * LANES:(g + 1) * LANES], cs)
        u_ref[:, g * LANES:(g + 1) * LANES] = res[:, 0:LANES].astype(BF16)
        u_ref[:, F_WIDTH + g * LANES:F_WIDTH + (g + 1) * LANES] = res[:, LANES:].astype(BF16)


def _inproj_body(x_ref, w_ref, wg_ref, bg_ref, trif_ref, trib_ref, cs_ref,
                 qf_ref, kf_ref, tf_ref, qb_ref, kb_ref, tb_ref, v_ref, r_ref, u_ref, decf_ref, decb_ref,
                 qk0, qk1, fx0, fx1, gt0, gt1, tot_ref):
    i = pl.program_id(0)

    @pl.when(i == 0)
    def _():
        qk1[...] = jnp.zeros_like(qk1)
        fx1[...] = jnp.zeros_like(fx1)
        gt1[...] = jnp.zeros_like(gt1)

    def step(cur, prev):
        stage_a = _inproj_project(x_ref, w_ref, v_ref, r_ref, *cur)
        stage_b = _inproj_decay(*prev, wg_ref, bg_ref, trif_ref, trib_ref, cs_ref,
                                qf_ref, kf_ref, tf_ref, qb_ref, kb_ref, tb_ref, u_ref, decf_ref, decb_ref, tot_ref)
        _interleave(stage_a, stage_b)

    @pl.when(lax.rem(i, 2) == 0)
    def _():
        step((qk0, fx0, gt0), (qk1, fx1, gt1))

    @pl.when(lax.rem(i, 2) == 1)
    def _():
        step((qk1, fx1, gt1), (qk0, fx0, gt0))


def _inproj(x2, w_in_r, wg, bg, trif, trib, cs):
    t = x2.shape[0]
    nt = t // TM_TOK
    cur = lambda i: (jnp.minimum(i, nt - 1), 0)
    prev = lambda i: (jnp.maximum(i - 1, 0), 0)
    full = lambda a: pl.BlockSpec(a.shape, lambda i: (0,) * a.ndim)
    kw = jax.ShapeDtypeStruct((t, GLA_KEY_WIDTH), BF16)
    dec = jax.ShapeDtypeStruct((t // CHUNK, GLA_KEY_WIDTH), F32)
    dec_spec = pl.BlockSpec((TM_TOK // CHUNK, GLA_KEY_WIDTH), prev)
    wide = jax.ShapeDtypeStruct((t, GLA_WIDTH), BF16)
    return pl.pallas_call(
        _inproj_body,
        grid=(nt + 1,),
        in_specs=[pl.BlockSpec((TM_TOK, D_MODEL), cur), full(w_in_r), full(wg), full(bg),
                  full(trif), full(trib), full(cs)],
        out_specs=[pl.BlockSpec((TM_TOK, GLA_KEY_WIDTH), prev)] * 6
                  + [pl.BlockSpec((TM_TOK, GLA_WIDTH), cur), pl.BlockSpec((TM_TOK, GLA_WIDTH), cur),
                     pl.BlockSpec((TM_TOK, 2 * F_WIDTH), prev), dec_spec, dec_spec],
        out_shape=[kw] * 6 + [wide, wide, jax.ShapeDtypeStruct((t, 2 * F_WIDTH), BF16), dec, dec],
        scratch_shapes=[pltpu.VMEM((TM_TOK, 2 * GLA_KEY_WIDTH), F32)] * 2
                       + [pltpu.VMEM((TM_TOK, F_WIDTH), BF16)] * 2
                       + [pltpu.VMEM((TM_TOK, GATE_COLS), BF16)] * 2
                       + [pltpu.VMEM((2 * GLA_KEY_WIDTH // LANES, TM_TOK, LANES), F32)],
        compiler_params=pltpu.CompilerParams(dimension_semantics=("arbitrary",), vmem_limit_bytes=VMEM_LIMIT),
        name="inproj",
    )(x2, w_in_r, wg, bg, trif, trib, cs)


def _gla_local(q_ref, k_ref, t_ref, v_ref, c, p, causal, m_lo, mv_lo, bd):
    rows = slice(c * CHUNK, (c + 1) * CHUNK)
    kl = slice(p * LANES, (p + 1) * LANES)
    vl = slice(p * 2 * GLA_DV, (p + 1) * 2 * GLA_DV)
    qd = q_ref[rows, kl]
    kd = k_ref[rows, kl]
    kt = t_ref[rows, kl]
    vv = v_ref[rows, vl]
    zk = jnp.zeros_like(kd)
    zv = jnp.zeros_like(vv)
    kbd = jnp.concatenate([jnp.where(m_lo, kd, zk), jnp.where(m_lo, zk, kd)], axis=0)
    att = _dot_nt(qd, kbd)
    att = jnp.where(causal, att, 0.0).astype(BF16)
    vbd = jnp.concatenate([jnp.where(mv_lo, vv, zv), jnp.where(mv_lo, zv, vv)], axis=0)
    kv = jnp.where(bd, _dot_tn(vv, kt), 0.0)
    return qd, att, vbd, kv


def _gla_body(qf_ref, kf_ref, tf_ref, vf_ref, df_ref, qb_ref, kb_ref, tb_ref, vb_ref, db_ref,
              of_ref, ob_ref, sf_ref, sb_ref):
    @pl.when(pl.program_id(1) == 0)
    def _():
        sf_ref[...] = jnp.zeros_like(sf_ref)
        sb_ref[...] = jnp.zeros_like(sb_ref)

    lane = lax.broadcasted_iota(jnp.int32, (CHUNK, LANES), 1)
    row = lax.broadcasted_iota(jnp.int32, (CHUNK, LANES), 0)
    m_lo = lane < GLA_DK
    col = lane & (CHUNK - 1)
    causal_f = row >= col
    causal_b = row <= col
    mv_lo = lax.broadcasted_iota(jnp.int32, (CHUNK, 2 * GLA_DV), 1) < GLA_DV
    bd = ((lax.broadcasted_iota(jnp.int32, (2 * GLA_DV, LANES), 0) < GLA_DV)
          == (lax.broadcasted_iota(jnp.int32, (2 * GLA_DV, LANES), 1) < GLA_DK))
    n = TM_TOK // CHUNK
    pairs = range(GLA_HEADS // 2)
    dirs = ((qf_ref, kf_ref, tf_ref, vf_ref, df_ref, of_ref, sf_ref, causal_f, lambda j: j),
            (qb_ref, kb_ref, tb_ref, vb_ref, db_ref, ob_ref, sb_ref, causal_b, lambda j: n - 1 - j))

    def local(j):
        return [[_gla_local(q, k, t, v, order(j), p, causal, m_lo, mv_lo, bd) for p in pairs]
                for (q, k, t, v, _, _, _, causal, order) in dirs]

    state = [[s_ref[p] for p in pairs] for (_, _, _, _, _, _, s_ref, _, _) in dirs]
    ahead = [local(j) for j in range(GLA_LOOKAHEAD)]
    for j in range(n):
        cur = ahead.pop(0)
        if j + GLA_LOOKAHEAD < n:
            ahead.append(local(j + GLA_LOOKAHEAD))
        for d, (_, _, _, _, d_ref, o_ref, _, _, order) in enumerate(dirs):
            c = order(j)
            dec = d_ref[c:c + 1, :]
            for p in pairs:
                qd, att, vbd, kv = cur[d][p]
                st = state[d][p]
                o = _dot(att, vbd) + _dot_nt(qd, st.astype(BF16))
                o_ref[c * CHUNK:(c + 1) * CHUNK, p * 2 * GLA_DV:(p + 1) * 2 * GLA_DV] = o.astype(o_ref.dtype)
                state[d][p] = st * dec[:, p * LANES:(p + 1) * LANES] + kv
    for d, (_, _, _, _, _, _, s_ref, _, _) in enumerate(dirs):
        for p in pairs:
            s_ref[p] = state[d][p]


def _gla(qf, kf, tf, qb, kb, tb, v, decf, decb, batch, seq):
    t = batch * seq
    nt = seq // TM_TOK
    fwd = lambda b, i: (b * nt + i, 0)
    bwd = lambda b, i: (b * nt + nt - 1 - i, 0)
    ks = lambda m: pl.BlockSpec((TM_TOK, GLA_KEY_WIDTH), m)
    vs = lambda m: pl.BlockSpec((TM_TOK, GLA_WIDTH), m)
    ds = lambda m: pl.BlockSpec((TM_TOK // CHUNK, GLA_KEY_WIDTH), m)
    o = jax.ShapeDtypeStruct((t, GLA_WIDTH), BF16)
    state = pltpu.VMEM((GLA_HEADS // 2, 2 * GLA_DV, LANES), F32)
    return pl.pallas_call(
        _gla_body,
        grid=(batch, nt),
        in_specs=[ks(fwd), ks(fwd), ks(fwd), vs(fwd), ds(fwd), ks(bwd), ks(bwd), ks(bwd), vs(bwd), ds(bwd)],
        out_specs=[vs(fwd), vs(bwd)],
        out_shape=[o, o],
        scratch_shapes=[state, state],
        compiler_params=pltpu.CompilerParams(dimension_semantics=("arbitrary", "arbitrary"),
                                             vmem_limit_bytes=VMEM_LIMIT),
        name="gla",
    )(qf, kf, tf, v, decf, qb, kb, tb, v, decb)


RADIX_ROWS = 512


def _radix4_body(z_ref, twc_ref, tws_ref, y_ref):
    z = [z_ref[s].astype(F32) for s in range(4)]
    re = [a[:, 0:F_WIDTH] for a in z]
    im = [a[:, F_WIDTH:] for a in z]
    ar, ai = re[0] + re[2], im[0] + im[2]
    br, bi = re[0] - re[2], im[0] - im[2]
    cr, ci = re[1] + re[3], im[1] + im[3]
    dr, di = re[1] - re[3], im[1] - im[3]
    y = [(ar + cr, ai + ci), (br + di, bi - dr), (ar - cr, ai - ci), (br - di, bi + dr)]
    y_ref[0, :, 0:F_WIDTH] = y[0][0].astype(BF16)
    y_ref[0, :, F_WIDTH:] = y[0][1].astype(BF16)
    for k1 in range(1, 4):
        c = jnp.concatenate([twc_ref[k1]] * (F_WIDTH // LANES), axis=1)
        s = jnp.concatenate([tws_ref[k1]] * (F_WIDTH // LANES), axis=1)
        yr, yi = y[k1]
        y_ref[k1, :, 0:F_WIDTH] = (yr * c + yi * s).astype(BF16)
        y_ref[k1, :, F_WIDTH:] = (yi * c - yr * s).astype(BF16)


def _radix4(u, batch):
    z = u.reshape(batch, 4, DFT_N, 2 * F_WIDTH)
    twc, tws = _twiddle_tables(4)
    twc, tws = jnp.asarray(twc), jnp.asarray(tws)
    nr = DFT_N // RADIX_ROWS
    blk = pl.BlockSpec((None, 4, RADIX_ROWS, 2 * F_WIDTH), lambda b, i: (b, 0, i, 0))
    tw = pl.BlockSpec((4, RADIX_ROWS, LANES), lambda b, i: (0, i, 0))
    y = pl.pallas_call(
        _radix4_body,
        grid=(batch, nr),
        in_specs=[blk, tw, tw],
        out_specs=blk,
        out_shape=jax.ShapeDtypeStruct((batch, 4, DFT_N, 2 * F_WIDTH), BF16),
        compiler_params=pltpu.CompilerParams(dimension_semantics=("arbitrary", "arbitrary"),
                                             vmem_limit_bytes=VMEM_LIMIT),
        name="radix4",
    )(z, twc, tws)
    return y


def _seqdft_body(cs_ref, edge_ref, mir_ref, ab_ref, o_ref, *, radix, scale):
    j = pl.program_id(1)
    first_row = lax.broadcasted_iota(jnp.int32, (DFT_ROWS, F_WIDTH), 0) == 0
    half = radix * DFT_ROWS
    rows = pl.ds(pl.multiple_of(j * DFT_ROWS, DFT_ROWS), DFT_ROWS)
    for k1 in range(radix):
        a = ab_ref[k1, :, 0:F_WIDTH]
        b = ab_ref[k1, :, F_WIDTH:]
        p = _dot(cs_ref[rows, 0:DFT_N], a)
        q = _dot(cs_ref[rows, DFT_N:], b)
        direct = (p + q) * scale
        edge = (_dot(edge_ref[:, 0:DFT_N], a) + _dot(edge_ref[:, DFT_N:], b)) * scale
        mirror = _dot(mir_ref[...], ((p - q) * scale).astype(BF16))
        mirror = jnp.where(first_row, jnp.where(j == 0, edge[1:2], edge[0:1]), mirror)
        for c in range(F_WIDTH // LANES):
            cols = slice(c * LANES, (c + 1) * LANES)
            o_ref[c, pl.ds(k1, DFT_ROWS, stride=radix), :] = direct[:, cols]
            o_ref[c, pl.ds(half + k1, DFT_ROWS, stride=radix), :] = mirror[:, cols]


def _seqdft_block(token_tile, tiles_per_batch, radix):
    b = token_tile // tiles_per_batch
    it = token_tile % tiles_per_batch
    ft = it // radix
    j = jnp.where(ft < 2, ft, 3 - ft)
    return ((b * 2 + j) * 2 + ft // 2) * radix + it % radix


def _seqdft(ab, cs, edge, mir, batch, radix):
    scale = float(1.0 / np.sqrt(radix * DFT_N))
    nj = DFT_N // (2 * DFT_ROWS)
    full = lambda a: pl.BlockSpec(a.shape, lambda b, j: (0,) * a.ndim)
    ab_bytes = radix * DFT_N * 2 * F_WIDTH * 2
    out_bytes = 2 * radix * DFT_ROWS * F_WIDTH * 4
    fixed_bytes = cs.size * 2 + 2 * (edge.size + mir.size) * 2 + 2 * out_bytes
    ab_buffers = 2 if fixed_bytes + 2 * ab_bytes <= VMEM_LIMIT - DFT_TEMP_BYTES else 1
    return pl.pallas_call(
        functools.partial(_seqdft_body, radix=radix, scale=scale),
        grid=(batch, nj),
        in_specs=[pl.BlockSpec(cs.shape, lambda b, j: (0, 0), pipeline_mode=pl.Buffered(1)), full(edge), full(mir),
                  pl.BlockSpec((None, radix, DFT_N, 2 * F_WIDTH), lambda b, j: (b, 0, 0, 0),
                               pipeline_mode=pl.Buffered(ab_buffers))],
        out_specs=pl.BlockSpec((F_WIDTH // LANES, 2 * radix * DFT_ROWS, LANES), lambda b, j: (0, b * nj + j, 0)),
        out_shape=jax.ShapeDtypeStruct((F_WIDTH // LANES, batch * radix * DFT_N, LANES), F32),
        compiler_params=pltpu.CompilerParams(dimension_semantics=("arbitrary", "arbitrary"),
                                             vmem_limit_bytes=VMEM_LIMIT),
        name="seqdft",
    )(cs, edge, mir, ab)


def _first_index(hit, rows):
    return jnp.min(jnp.where(hit, rows.astype(F32), 1e6), axis=0, keepdims=True).astype(jnp.int32)


def _postmix_body(of_ref, ob_ref, r_ref, fo_ref, x_ref, wo_ref, gg_ref, g2_ref, wr_ref,
                  h_ref, xn_ref, eidx_ref, cw_ref, rank_ref, cnt_ref, carry_ref):
    @pl.when(pl.program_id(0) == 0)
    def _():
        carry_ref[...] = jnp.zeros_like(carry_ref)

    o = of_ref[...].astype(F32) + ob_ref[...].astype(F32)
    r = r_ref[...].astype(F32)
    parts = []
    for hd in range(GLA_HEADS):
        sl = slice(hd * GLA_DV, (hd + 1) * GLA_DV)
        oh = o[:, sl]
        oh = oh * lax.rsqrt(jnp.mean(oh * oh, axis=-1, keepdims=True) + EPS)
        rh = r[:, sl]
        parts.append((oh * gg_ref[...] * (rh * jax.nn.sigmoid(rh))).astype(BF16))
    on = jnp.concatenate(parts, axis=1)
    fo = jnp.concatenate([fo_ref[c] for c in range(F_WIDTH // LANES)], axis=1)
    mixed = _dot(on, wo_ref[0:GLA_WIDTH, :]) + _dot(fo.astype(BF16), wo_ref[GLA_WIDTH:, :])
    h = x_ref[...] + mixed
    h_ref[...] = h
    xn = h * lax.rsqrt(jnp.mean(h * h, axis=-1, keepdims=True) + EPS) * g2_ref[...]
    _store_token_tiles(xn_ref, xn)
    logits = _dot_nt(wr_ref[...], xn.astype(BF16))

    sub8 = lax.broadcasted_iota(jnp.int32, (8, TM_TOK), 0)
    lg = jnp.where(sub8 < N_GROUPS, logits[0:8], -jnp.inf)
    gmax = jnp.max(lg, axis=0, keepdims=True)
    g_w = 1.0 / jnp.sum(jnp.exp(lg - gmax), axis=0, keepdims=True)
    g_sel = _first_index(lg == gmax, sub8)
    sel = logits[8:16]
    for g in range(1, N_GROUPS):
        sel = jnp.where(g_sel == g, logits[8 + 8 * g:16 + 8 * g], sel)
    m1 = jnp.max(sel, axis=0, keepdims=True)
    i1 = _first_index(sel == m1, sub8)
    sel2 = jnp.where(sub8 == i1, -jnp.inf, sel)
    m2 = jnp.max(sel2, axis=0, keepdims=True)
    i2 = _first_index(sel2 == m2, sub8)
    e21 = jnp.exp(m2 - m1)
    w1 = 1.0 / (1.0 + e21)
    w2 = e21 / (1.0 + e21)
    e1 = g_sel * EXPERTS_PER_GROUP + i1
    e2 = g_sel * EXPERTS_PER_GROUP + i2
    eidx_ref[...] = jnp.concatenate([e1, e2], axis=0)
    cw_ref[...] = jnp.concatenate([g_w * w1, g_w * w2], axis=0)

    sub = lax.broadcasted_iota(jnp.int32, (N_EXPERTS, TM_TOK), 0)
    oh1 = sub == e1
    oh2 = sub == e2
    oh1b = jnp.where(oh1, 1.0, 0.0).astype(BF16)
    oh2b = jnp.where(oh2, 1.0, 0.0).astype(BF16)
    before = (lax.broadcasted_iota(jnp.int32, (TM_TOK, TM_TOK), 0)
              < lax.broadcasted_iota(jnp.int32, (TM_TOK, TM_TOK), 1))
    before = jnp.where(before, 1.0, 0.0).astype(BF16)
    ones = jnp.ones((TM_TOK, LANES), BF16)
    oh12 = jnp.concatenate([oh1b, oh2b], axis=0)
    p12 = _dot(oh12, before)
    c12 = _dot(oh12, ones)
    p1, p2 = p12[0:N_EXPERTS], p12[N_EXPERTS:]
    c1, c2 = c12[0:N_EXPERTS], c12[N_EXPERTS:]
    carry = carry_ref[...]
    rep = TM_TOK // LANES
    base1 = jnp.concatenate([carry] * rep, axis=1)
    base2 = jnp.concatenate([carry + c1] * rep, axis=1)
    rk1 = jnp.sum(jnp.where(oh1, p1 + base1, 0.0), axis=0, keepdims=True)
    rk2 = jnp.sum(jnp.where(oh2, p2 + base2, 0.0), axis=0, keepdims=True)
    rank_ref[...] = jnp.concatenate([rk1, rk2], axis=0).astype(jnp.int32)
    carry = carry + c1 + c2
    carry_ref[...] = carry
    cnt_ref[...] = carry


def _postmix(of, ob, r, fo, fo_block, x2, wo, gg, g2, wr):
    t = x2.shape[0]
    nt = t // TM_TOK
    tok = lambda w: pl.BlockSpec((TM_TOK, w), lambda i: (i, 0))
    full = lambda a: pl.BlockSpec(a.shape, lambda i: (0,) * a.ndim)
    lane2 = pl.BlockSpec((2, TM_TOK), lambda i: (0, i))
    return pl.pallas_call(
        _postmix_body,
        grid=(nt,),
        in_specs=[tok(GLA_WIDTH), tok(GLA_WIDTH), tok(GLA_WIDTH),
                  pl.BlockSpec((F_WIDTH // LANES, TM_TOK, LANES), lambda i: (0, fo_block(i), 0)), tok(D_MODEL),
                  full(wo), full(gg), full(g2), full(wr)],
        out_specs=[tok(D_MODEL), pl.BlockSpec((TM_TOK * TOK_ROWS, LANES), lambda i: (i, 0)), lane2, lane2, lane2,
                   pl.BlockSpec((N_EXPERTS, LANES), lambda i: (0, 0))],
        out_shape=[jax.ShapeDtypeStruct((t, D_MODEL), F32), jax.ShapeDtypeStruct((t * TOK_ROWS, LANES), F32),
                   jax.ShapeDtypeStruct((2, t), jnp.int32), jax.ShapeDtypeStruct((2, t), F32),
                   jax.ShapeDtypeStruct((2, t), jnp.int32), jax.ShapeDtypeStruct((N_EXPERTS, LANES), F32)],
        scratch_shapes=[pltpu.VMEM((N_EXPERTS, LANES), F32)],
        compiler_params=pltpu.CompilerParams(dimension_semantics=("arbitrary",), vmem_limit_bytes=VMEM_LIMIT),
        name="postmix",
    )(of, ob, r, fo, x2, wo, gg, g2, wr)


def _store_token_tiles(ref, val):
    n = val.shape[0]
    for c in range(TOK_ROWS):
        ref[pl.ds(c, n, stride=TOK_ROWS), :] = val[:, c * LANES:(c + 1) * LANES]


def _load_token_tiles(ref, n):
    return jnp.concatenate([ref[pl.ds(c, n, stride=TOK_ROWS), :] for c in range(TOK_ROWS)], axis=1)


def _token_rows(tok):
    if isinstance(tok, int):
        return pl.ds(tok * TOK_ROWS, TOK_ROWS)
    return pl.ds(pl.multiple_of(tok * TOK_ROWS, TOK_ROWS), TOK_ROWS)


def _row_copy(src_hbm, dst_hbm, src_tok, dst_tok, sem):
    return pltpu.make_async_copy(src_hbm.at[_token_rows(src_tok)], dst_hbm.at[_token_rows(dst_tok)], sem)


def _dispatch_body(fill_ref, pos_ref, xn_ref, xs_hbm, zeros_ref, fill_sem, row_sem):
    i = pl.program_id(0)

    @pl.when(i == 0)
    def _():
        zeros_ref[...] = jnp.zeros_like(zeros_ref)
        tile_rows = TM_EXP * TOK_ROWS

        def fill(e):
            rows = pl.ds(pl.multiple_of(fill_ref[e] * tile_rows, tile_rows), tile_rows)
            return pltpu.make_async_copy(zeros_ref, xs_hbm.at[rows], fill_sem)

        for e in range(N_FILL):
            @pl.when(fill_ref[e] >= 0)
            def _():
                fill(e).start()
        for e in range(N_FILL):
            @pl.when(fill_ref[e] >= 0)
            def _():
                fill(e).wait()

    for j in range(TM_DISP):
        for s in range(2):
            _row_copy(xn_ref, xs_hbm, j, pos_ref[s, j], row_sem).start(priority=s)
    for j in range(TM_DISP):
        for s in range(2):
            _row_copy(xn_ref, xs_hbm, j, 0, row_sem).wait()


def _dispatch(xn, pos, fill_rows, rows_total):
    t = xn.shape[0] // TOK_ROWS
    grid_spec = pltpu.PrefetchScalarGridSpec(
        num_scalar_prefetch=1,
        grid=(t // TM_DISP,),
        in_specs=[pl.BlockSpec((2, TM_DISP), lambda i, fill: (0, i), memory_space=pltpu.SMEM),
                  pl.BlockSpec((TM_DISP * TOK_ROWS, LANES), lambda i, fill: (i, 0))],
        out_specs=pl.BlockSpec(memory_space=pl.ANY),
        scratch_shapes=[pltpu.VMEM((TM_EXP * TOK_ROWS, LANES), F32), pltpu.SemaphoreType.DMA(()),
                        pltpu.SemaphoreType.DMA(())],
    )
    return pl.pallas_call(
        _dispatch_body,
        grid_spec=grid_spec,
        out_shape=jax.ShapeDtypeStruct((rows_total * TOK_ROWS, LANES), F32),
        compiler_params=pltpu.CompilerParams(dimension_semantics=("arbitrary",), vmem_limit_bytes=VMEM_LIMIT),
        name="dispatch",
    )(fill_rows, pos, xn)


def _experts_body(te_ref, na_ref, xs_ref, wg_ref, wu_ref, wd_ref, ys_ref, wg_s, wu_s, wd_s):
    i = pl.program_id(0)
    active = i < na_ref[0]

    @pl.when(active & ((i == 0) | (te_ref[i] != te_ref[jnp.maximum(i - 1, 0)])))
    def _():
        wg_s[...] = wg_ref[...].astype(BF16)
        wu_s[...] = wu_ref[...].astype(BF16)
        wd_s[...] = wd_ref[...].astype(BF16)

    @pl.when(active)
    def _():
        x = _load_token_tiles(xs_ref, TM_EXP).astype(BF16)
        gate = _dot(x, wg_s[...])
        up = _dot(x, wu_s[...])
        hid = (gate * jax.nn.sigmoid(gate) * up).astype(BF16)
        _store_token_tiles(ys_ref, _dot(hid, wd_s[...]))

    @pl.when(jnp.logical_not(active))
    def _():
        ys_ref[...] = jnp.zeros_like(ys_ref)


def _experts(xs, tile_expert, n_active, wg, wu, wd):
    n_tiles = xs.shape[0] // (TM_EXP * TOK_ROWS)
    row = lambda i, te, na: (i, 0)
    row_in = lambda i, te, na: (jnp.minimum(i, na[0] - 1), 0)
    wsel = lambda i, te, na: (te[jnp.minimum(i, na[0] - 1)], 0, 0)
    grid_spec = pltpu.PrefetchScalarGridSpec(
        num_scalar_prefetch=2,
        grid=(n_tiles,),
        in_specs=[pl.BlockSpec((TM_EXP * TOK_ROWS, LANES), row_in),
                  pl.BlockSpec((None, D_MODEL, D_EXPERT), wsel),
                  pl.BlockSpec((None, D_MODEL, D_EXPERT), wsel),
                  pl.BlockSpec((None, D_EXPERT, D_MODEL), wsel)],
        out_specs=pl.BlockSpec((TM_EXP * TOK_ROWS, LANES), row),
        scratch_shapes=[pltpu.VMEM((D_MODEL, D_EXPERT), BF16), pltpu.VMEM((D_MODEL, D_EXPERT), BF16),
                        pltpu.VMEM((D_EXPERT, D_MODEL), BF16)],
    )
    return pl.pallas_call(
        _experts_body,
        grid_spec=grid_spec,
        out_shape=jax.ShapeDtypeStruct(xs.shape, F32),
        compiler_params=pltpu.CompilerParams(dimension_semantics=("arbitrary",), vmem_limit_bytes=VMEM_LIMIT),
        name="experts",
    )(tile_expert, n_active, xs, wg, wu, wd)


def _combine_body(pos_ref, nxt_ref, ys_hbm, cw_ref, h_ref, gf_ref, y_ref, buf_ref, sem):
    i = pl.program_id(0)
    slot = lax.rem(i, 2)

    def row_gather(p_ref, sl, s, j):
        return pltpu.make_async_copy(ys_hbm.at[_token_rows(p_ref[s, j])], buf_ref.at[sl, s, _token_rows(j)],
                                     sem.at[sl])

    def issue_tile(p_ref, sl):
        for j in range(TM_COMB):
            for s in range(2):
                row_gather(p_ref, sl, s, j).start(priority=s)

    @pl.when(i == 0)
    def _():
        issue_tile(pos_ref, slot)

    @pl.when(i + 1 < pl.num_programs(0))
    def _():
        issue_tile(nxt_ref, 1 - slot)

    for j in range(TM_COMB):
        for s in range(2):
            pltpu.make_async_copy(ys_hbm.at[_token_rows(0)], buf_ref.at[slot, s, _token_rows(j)], sem.at[slot]).wait()

    cw = cw_ref[...]
    y0 = _load_token_tiles(buf_ref.at[slot, 0], TM_COMB)
    y1 = _load_token_tiles(buf_ref.at[slot, 1], TM_COMB)
    h = h_ref[...] + cw[:, 0:1] * y0 + cw[:, 1:2] * y1
    y_ref[...] = h * lax.rsqrt(jnp.mean(h * h, axis=-1, keepdims=True) + EPS) * gf_ref[...]


def _combine(ys, pos, cw_t, h, gf):
    t = h.shape[0]
    n = t // TM_COMB
    tok = pl.BlockSpec((TM_COMB, D_MODEL), lambda i: (i, 0))
    return pl.pallas_call(
        _combine_body,
        grid=(n,),
        in_specs=[pl.BlockSpec((2, TM_COMB), lambda i: (0, i), memory_space=pltpu.SMEM),
                  pl.BlockSpec((2, TM_COMB), lambda i: (0, jnp.minimum(i + 1, n - 1)), memory_space=pltpu.SMEM),
                  pl.BlockSpec(memory_space=pl.ANY),
                  pl.BlockSpec((TM_COMB, 2), lambda i: (i, 0)),
                  tok,
                  pl.BlockSpec((1, D_MODEL), lambda i: (0, 0))],
        out_specs=tok,
        out_shape=jax.ShapeDtypeStruct((t, D_MODEL), F32),
        scratch_shapes=[pltpu.VMEM((2, 2, TM_COMB * TOK_ROWS, LANES), F32), pltpu.SemaphoreType.DMA((2,))],
        compiler_params=pltpu.CompilerParams(dimension_semantics=("arbitrary",), vmem_limit_bytes=VMEM_LIMIT),
        name="combine",
    )(pos, pos, ys, cw_t, h, gf)


def _prepare_params(norm1_g, w_in, w_gk2_f, b_gk_f, w_gk2_b, b_gk_b, gla_norm_g, w_out,
                    norm2_g, w_group, w_expert, w_gate, w_up, w_down, norm_f_g):
    w = w_in[0] * norm1_g[0][:, None]
    gate_lo = 2 * GLA_KEY_WIDTH + 2 * GLA_WIDTH
    gate_hi = gate_lo + 2 * GATE_RANK
    w_in_r = jnp.concatenate([w[:, :gate_lo], w[:, gate_hi:], w[:, gate_lo:gate_hi],
                              jnp.zeros((D_MODEL, GATE_COLS - 2 * GATE_RANK), F32)], axis=1).astype(BF16)
    zk = jnp.zeros((GATE_RANK, GLA_KEY_WIDTH), F32)
    wg = jnp.concatenate([jnp.concatenate([w_gk2_f[0], zk], axis=1), jnp.concatenate([zk, w_gk2_b[0]], axis=1),
                          jnp.zeros((GATE_COLS - 2 * GATE_RANK, 2 * GLA_KEY_WIDTH), F32)], axis=0).astype(BF16)
    bg = jnp.concatenate([b_gk_f[0], b_gk_b[0]])[None, :]
    wr = jnp.concatenate([w_group[0].T, jnp.zeros((8 - N_GROUPS, D_MODEL), F32), w_expert[0].T,
                          jnp.zeros((ROUTER_ROWS - 8 - N_EXPERTS, D_MODEL), F32)], axis=0).astype(BF16)
    return dict(
        w_in_r=w_in_r, wg=wg, bg=bg,
        gg=gla_norm_g[0][None, :], wo=w_out[0].astype(BF16), g2=norm2_g[0][None, :], wr=wr,
        w_gate=w_gate[0], w_up=w_up[0], w_down=w_down[0],
        gf=norm_f_g[None, :],
        trif=jnp.asarray(_TRI_F).astype(BF16), trib=jnp.asarray(_TRI_B).astype(BF16),
        chan_dft=jnp.asarray(_CHAN_DFT).astype(BF16), seq_dft=jnp.asarray(_SEQ_DFT_LOWER).astype(BF16),
        seq_edge=jnp.asarray(_SEQ_DFT_EDGE).astype(BF16), seq_mirror=jnp.asarray(_SEQ_DFT_MIRROR).astype(BF16))


def _encoder(x, p):
    batch, seq, _ = x.shape
    t = batch * seq
    assert seq % DFT_N == 0 and seq // DFT_N in (1, 4), "sequence DFT supports seq = 2048 or 8192"
    radix = seq // DFT_N
    x2 = x.reshape(t, D_MODEL)
    qf, kf, tf, qb, kb, tb, v, r, u, decf, decb = _inproj(x2, p["w_in_r"], p["wg"], p["bg"],
                                                          p["trif"], p["trib"], p["chan_dft"])
    of, ob = _gla(qf, kf, tf, qb, kb, tb, v, decf, decb, batch, seq)
    if radix == 1:
        ab = u.reshape(batch, 1, DFT_N, 2 * F_WIDTH)
    else:
        ab = _radix4(u, batch)
    fo = _seqdft(ab, p["seq_dft"], p["seq_edge"], p["seq_mirror"], batch, radix)
    fo_block = functools.partial(_seqdft_block, tiles_per_batch=seq // TM_TOK, radix=radix)
    h, xn, eidx, cw, rank, cnt = _postmix(of, ob, r, fo, fo_block, x2, p["wo"], p["gg"], p["g2"], p["wr"])

    counts = cnt[:, 0].astype(jnp.int32)
    tiles = (counts + TM_EXP - 1) // TM_EXP
    tile_end = jnp.cumsum(tiles)
    tile_start = tile_end - tiles
    experts = jnp.arange(N_EXPERTS, dtype=jnp.int32)
    seg_row = jnp.sum(jnp.where(eidx[:, :, None] == experts, tile_start * TM_EXP, 0), axis=-1)
    pos = seg_row + rank
    n_tiles = 2 * t // TM_EXP + N_EXPERTS
    tile_ids = jnp.arange(n_tiles, dtype=jnp.int32)
    tile_expert = jnp.minimum(jnp.sum((tile_end[None, :] <= tile_ids[:, None]).astype(jnp.int32), axis=1),
                              N_EXPERTS - 1)
    n_active = tile_end[-1:].astype(jnp.int32)
    tail = n_active + jnp.arange(N_EXPERTS, dtype=jnp.int32)
    fill_tiles = jnp.concatenate([jnp.where(tiles > 0, tile_end - 1, -1),
                                  jnp.where(tail < n_tiles, tail, -1)]).astype(jnp.int32)

    xs = _dispatch(xn, pos, fill_tiles, n_tiles * TM_EXP)
    ys = _experts(xs, tile_expert, n_active, p["w_gate"], p["w_up"], p["w_down"])
    y = _combine(ys, pos, cw.T, h, p["gf"])
    return y.reshape(batch, seq, D_MODEL)


def kernel(x_prompt, x_sample, norm1_g, w_in, w_gk2_f, b_gk_f, w_gk2_b, b_gk_b, gla_norm_g, w_out, norm2_g,
           w_group, w_expert, w_gate, w_up, w_down, norm_f_g):
    p = _prepare_params(norm1_g, w_in, w_gk2_f, b_gk_f, w_gk2_b, b_gk_b, gla_norm_g, w_out,
                        norm2_g, w_group, w_expert, w_gate, w_up, w_down, norm_f_g)
    return (_encoder(x_prompt, p), _encoder(x_sample, p))
```

```python
import functools

import numpy as np
import jax
import jax.numpy as jnp
from jax import lax
from jax.experimental import pallas as pl
from jax.experimental.pallas import tpu as pltpu

D_MODEL = 1024
EPS = 1e-6
GLA_HEADS = 4
GLA_DV = 128
GLA_DK = 64
GLA_WIDTH = GLA_HEADS * GLA_DV
GLA_KEY_WIDTH = GLA_HEADS * GLA_DK
GATE_RANK = 16
GATE_NORMALIZER = 16.0
CHUNK = 64
F_GROUPS = 4
F_GROUP_DIM = 128
F_WIDTH = F_GROUPS * F_GROUP_DIM
N_GROUPS = 4
EXPERTS_PER_GROUP = 8
N_EXPERTS = N_GROUPS * EXPERTS_PER_GROUP
D_EXPERT = 256

LANES = 128
V7X_VMEM_BYTES = 64 * 1024 * 1024
VMEM_LIMIT = 56 * 1024 * 1024

TM_TOK = 512
SUB = 128
DFT_N = 2048
DFT_ROWS = 512
DFT_TEMP_BYTES = 5 * 512 * 512 * 4
TM_EXP = 512
TM_DISP = 1024
TM_COMB = 256
GLA_LOOKAHEAD = 2
ROUTER_ROWS = 48
GATE_COLS = 128
IN_COLS_PAD = 2048 + GATE_COLS
TOK_ROWS = D_MODEL // LANES
N_FILL = 2 * N_EXPERTS

BF16 = jnp.bfloat16
F32 = jnp.float32


def _dot(a, b):
    return jnp.dot(a, b, preferred_element_type=F32)


def _interleave(*stages):
    live = list(stages)
    while live:
        for g in list(live):
            try:
                next(g)
            except StopIteration:
                live.remove(g)


def _dot_nt(a, b):
    return lax.dot_general(a, b, (((1,), (1,)), ((), ())), preferred_element_type=F32)


def _dot_tn(a, b):
    return lax.dot_general(a, b, (((0,), (0,)), ((), ())), preferred_element_type=F32)


def _tri_tables():
    r = np.arange(SUB)
    same = (r[:, None] // CHUNK) == (r[None, :] // CHUNK)
    l_incl = same & (r[None, :] <= r[:, None])
    u_strict = same & (r[None, :] > r[:, None])
    u_incl = same & (r[None, :] >= r[:, None])
    l_strict = same & (r[None, :] < r[:, None])
    fwd = np.concatenate([l_incl, u_strict], 0).astype(np.float32)
    bwd = np.concatenate([u_incl, l_strict], 0).astype(np.float32)
    return fwd, bwd


def _chan_dft_table():
    c = np.arange(F_GROUP_DIM)
    ang = 2.0 * np.pi * ((c[:, None] * c[None, :]) % F_GROUP_DIM) / F_GROUP_DIM
    s = 1.0 / np.sqrt(F_GROUP_DIM)
    return np.concatenate([np.cos(ang) * s, -np.sin(ang) * s], 1).astype(np.float32)


def _seq_dft_tables():
    k = np.arange(DFT_N // 2, dtype=np.int64)
    s = np.arange(DFT_N, dtype=np.int64)
    ang = 2.0 * np.pi * ((k[:, None] * s[None, :]) % DFT_N) / DFT_N
    lower = np.concatenate([np.cos(ang), np.sin(ang)], 1).astype(np.float32)
    edge = np.zeros((16, 2 * DFT_N), np.float32)
    edge[0, :DFT_N] = 1.0 - 2.0 * (s % 2)
    edge[1, :DFT_N] = np.array([1.0, 0.0, -1.0, 0.0])[s % 4]
    edge[1, DFT_N:] = -np.array([0.0, 1.0, 0.0, -1.0])[s % 4]
    i = np.arange(DFT_ROWS)
    mirror = ((i[:, None] >= 1) & (i[None, :] == DFT_ROWS - i[:, None])).astype(np.float32)
    return lower, edge, mirror


def _twiddle_tables(radix):
    k1 = np.arange(radix, dtype=np.int64)[:, None]
    s2 = np.arange(DFT_N, dtype=np.int64)[None, :]
    ang = 2.0 * np.pi * ((k1 * s2) % (radix * DFT_N)) / (radix * DFT_N)
    c = np.repeat(np.cos(ang)[:, :, None], LANES, 2).astype(np.float32)
    s = np.repeat(np.sin(ang)[:, :, None], LANES, 2).astype(np.float32)
    return c, s


_TRI_F, _TRI_B = _tri_tables()
_CHAN_DFT = _chan_dft_table()
_SEQ_DFT_LOWER, _SEQ_DFT_EDGE, _SEQ_DFT_MIRROR = _seq_dft_tables()


def _inproj_project(x_ref, w_ref, v_ref, r_ref, qk_s, fx_s, gt_s):
    x = x_ref[...]
    inv = lax.rsqrt(jnp.mean(x * x, axis=-1, keepdims=True) + EPS)
    xb = x.astype(BF16)
    yield
    qk_s[...] = _dot(xb, w_ref[:, 0:2 * GLA_KEY_WIDTH]) * inv
    yield
    v_ref[...] = (_dot(xb, w_ref[:, 512:1024]) * inv).astype(BF16)
    yield
    r_ref[...] = (_dot(xb, w_ref[:, 1024:1536]) * inv).astype(BF16)
    yield
    fx_s[...] = (_dot(xb, w_ref[:, 1536:2048]) * inv).astype(BF16)
    yield
    gt_s[...] = (_dot(xb, w_ref[:, 2048:IN_COLS_PAD]) * inv).astype(BF16)


def _inproj_decay(qk_s, fx_s, gt_s, wg_ref, bg_ref, trif_ref, trib_ref, cs_ref,
                  qf_ref, kf_ref, tf_ref, qb_ref, kb_ref, tb_ref, u_ref, decf_ref, decb_ref, tot_ref):
    z = _dot(gt_s[...], wg_ref[...]) + bg_ref[...]
    la = (jnp.minimum(z, 0.0) - jnp.log1p(jnp.exp(-jnp.abs(z)))) * (1.0 / GATE_NORMALIZER)
    la_hi = la.astype(BF16)
    la_lo = (la - la_hi.astype(F32)).astype(BF16)
    trif = trif_ref[...]
    trib = trib_ref[...]
    scale = GLA_DK ** -0.5
    for s in range(TM_TOK // SUB):
        yield
        rows = slice(s * SUB, (s + 1) * SUB)
        q = qk_s[rows, 0:GLA_KEY_WIDTH]
        k = qk_s[rows, GLA_KEY_WIDTH:2 * GLA_KEY_WIDTH]
        rf = _dot(trif, la_hi[rows, 0:GLA_KEY_WIDTH]) + _dot(trif, la_lo[rows, 0:GLA_KEY_WIDTH])
        b, tl = rf[0:SUB], rf[SUB:2 * SUB]
        qf_ref[rows, :] = (q * scale * jnp.exp(b)).astype(BF16)
        kf_ref[rows, :] = (k * jnp.exp(-b)).astype(BF16)
        tf_ref[rows, :] = (k * jnp.exp(tl)).astype(BF16)
        totf = b + tl
        tot_ref[0, rows, :] = totf[:, 0:LANES]
        tot_ref[1, rows, :] = totf[:, LANES:]
        rb = _dot(trib, la_hi[rows, GLA_KEY_WIDTH:]) + _dot(trib, la_lo[rows, GLA_KEY_WIDTH:])
        c, tlb = rb[0:SUB], rb[SUB:2 * SUB]
        qb_ref[rows, :] = (q * scale * jnp.exp(c)).astype(BF16)
        kb_ref[rows, :] = (k * jnp.exp(-c)).astype(BF16)
        tb_ref[rows, :] = (k * jnp.exp(tlb)).astype(BF16)
        totb = c + tlb
        tot_ref[2, rows, :] = totb[:, 0:LANES]
        tot_ref[3, rows, :] = totb[:, LANES:]
    yield
    chunk_rows = pl.ds(0, TM_TOK // CHUNK, stride=CHUNK)
    decf_ref[:, 0:LANES] = jnp.exp(tot_ref[0, chunk_rows, :])
    decf_ref[:, LANES:] = jnp.exp(tot_ref[1, chunk_rows, :])
    decb_ref[:, 0:LANES] = jnp.exp(tot_ref[2, chunk_rows, :])
    decb_ref[:, LANES:] = jnp.exp(tot_ref[3, chunk_rows, :])
    cs = cs_ref[...]
    for g in range(F_GROUPS):
        res = _dot(fx_s[:, g * LANES:(g + 1) * LANES], cs)
        u_ref[:, g * LANES:(g + 1) * LANES] = res[:, 0:LANES].astype(BF16)
        u_ref[:, F_WIDTH + g * LANES:F_WIDTH + (g + 1) * LANES] = res[:, LANES:].astype(BF16)


def _inproj_body(x_ref, w_ref, wg_ref, bg_ref, trif_ref, trib_ref, cs_ref,
                 qf_ref, kf_ref, tf_ref, qb_ref, kb_ref, tb_ref, v_ref, r_ref, u_ref, decf_ref, decb_ref,
                 qk0, qk1, fx0, fx1, gt0, gt1, tot_ref):
    i = pl.program_id(0)

    @pl.when(i == 0)
    def _():
        qk1[...] = jnp.zeros_like(qk1)
        fx1[...] = jnp.zeros_like(fx1)
        gt1[...] = jnp.zeros_like(gt1)

    def step(cur, prev):
        stage_a = _inproj_project(x_ref, w_ref, v_ref, r_ref, *cur)
        stage_b = _inproj_decay(*prev, wg_ref, bg_ref, trif_ref, trib_ref, cs_ref,
                                qf_ref, kf_ref, tf_ref, qb_ref, kb_ref, tb_ref, u_ref, decf_ref, decb_ref, tot_ref)
        _interleave(stage_a, stage_b)

    @pl.when(lax.rem(i, 2) == 0)
    def _():
        step((qk0, fx0, gt0), (qk1, fx1, gt1))

    @pl.when(lax.rem(i, 2) == 1)
    def _():
        step((qk1, fx1, gt1), (qk0, fx0, gt0))


def _inproj(x2, w_in_r, wg, bg, trif, trib, cs):
    t = x2.shape[0]
    nt = t // TM_TOK
    cur = lambda i: (jnp.minimum(i, nt - 1), 0)
    prev = lambda i: (jnp.maximum(i - 1, 0), 0)
    full = lambda a: pl.BlockSpec(a.shape, lambda i: (0,) * a.ndim)
    kw = jax.ShapeDtypeStruct((t, GLA_KEY_WIDTH), BF16)
    dec = jax.ShapeDtypeStruct((t // CHUNK, GLA_KEY_WIDTH), F32)
    dec_spec = pl.BlockSpec((TM_TOK // CHUNK, GLA_KEY_WIDTH), prev)
    wide = jax.ShapeDtypeStruct((t, GLA_WIDTH), BF16)
    return pl.pallas_call(
        _inproj_body,
        grid=(nt + 1,),
        in_specs=[pl.BlockSpec((TM_TOK, D_MODEL), cur), full(w_in_r), full(wg), full(bg),
                  full(trif), full(trib), full(cs)],
        out_specs=[pl.BlockSpec((TM_TOK, GLA_KEY_WIDTH), prev)] * 6
                  + [pl.BlockSpec((TM_TOK, GLA_WIDTH), cur), pl.BlockSpec((TM_TOK, GLA_WIDTH), cur),
                     pl.BlockSpec((TM_TOK, 2 * F_WIDTH), prev), dec_spec, dec_spec],
        out_shape=[kw] * 6 + [wide, wide, jax.ShapeDtypeStruct((t, 2 * F_WIDTH), BF16), dec, dec],
        scratch_shapes=[pltpu.VMEM((TM_TOK, 2 * GLA_KEY_WIDTH), F32)] * 2
                       + [pltpu.VMEM((TM_TOK, F_WIDTH), BF16)] * 2
                       + [pltpu.VMEM((TM_TOK, GATE_COLS), BF16)] * 2
                       + [pltpu.VMEM((2 * GLA_KEY_WIDTH // LANES, TM_TOK, LANES), F32)],
        compiler_params=pltpu.CompilerParams(dimension_semantics=("arbitrary",), vmem_limit_bytes=VMEM_LIMIT),
        name="inproj",
    )(x2, w_in_r, wg, bg, trif, trib, cs)


def _gla_local(q_ref, k_ref, t_ref, v_ref, c, p, causal, m_lo, mv_lo, bd):
    rows = slice(c * CHUNK, (c + 1) * CHUNK)
    kl = slice(p * LANES, (p + 1) * LANES)
    vl = slice(p * 2 * GLA_DV, (p + 1) * 2 * GLA_DV)
    qd = q_ref[rows, kl]
    kd = k_ref[rows, kl]
    kt = t_ref[rows, kl]
    vv = v_ref[rows, vl]
    zk = jnp.zeros_like(kd)
    zv = jnp.zeros_like(vv)
    kbd = jnp.concatenate([jnp.where(m_lo, kd, zk), jnp.where(m_lo, zk, kd)], axis=0)
    att = _dot_nt(qd, kbd)
    att = jnp.where(causal, att, 0.0).astype(BF16)
    vbd = jnp.concatenate([jnp.where(mv_lo, vv, zv), jnp.where(mv_lo, zv, vv)], axis=0)
    kv = jnp.where(bd, _dot_tn(vv, kt), 0.0)
    return qd, att, vbd, kv


def _gla_body(qf_ref, kf_ref, tf_ref, vf_ref, df_ref, qb_ref, kb_ref, tb_ref, vb_ref, db_ref,
              of_ref, ob_ref, sf_ref, sb_ref):
    @pl.when(pl.program_id(1) == 0)
    def _():
        sf_ref[...] = jnp.zeros_like(sf_ref)
        sb_ref[...] = jnp.zeros_like(sb_ref)

    lane = lax.broadcasted_iota(jnp.int32, (CHUNK, LANES), 1)
    row = lax.broadcasted_iota(jnp.int32, (CHUNK, LANES), 0)
    m_lo = lane < GLA_DK
    col = lane & (CHUNK - 1)
    causal_f = row >= col
    causal_b = row <= col
    mv_lo = lax.broadcasted_iota(jnp.int32, (CHUNK, 2 * GLA_DV), 1) < GLA_DV
    bd = ((lax.broadcasted_iota(jnp.int32, (2 * GLA_DV, LANES), 0) < GLA_DV)
          == (lax.broadcasted_iota(jnp.int32, (2 * GLA_DV, LANES), 1) < GLA_DK))
    n = TM_TOK // CHUNK
    pairs = range(GLA_HEADS // 2)
    dirs = ((qf_ref, kf_ref, tf_ref, vf_ref, df_ref, of_ref, sf_ref, causal_f, lambda j: j),
            (qb_ref, kb_ref, tb_ref, vb_ref, db_ref, ob_ref, sb_ref, causal_b, lambda j: n - 1 - j))

    def local(j):
        return [[_gla_local(q, k, t, v, order(j), p, causal, m_lo, mv_lo, bd) for p in pairs]
                for (q, k, t, v, _, _, _, causal, order) in dirs]

    state = [[s_ref[p] for p in pairs] for (_, _, _, _, _, _, s_ref, _, _) in dirs]
    ahead = [local(j) for j in range(GLA_LOOKAHEAD)]
    for j in range(n):
        cur = ahead.pop(0)
        if j + GLA_LOOKAHEAD < n:
            ahead.append(local(j + GLA_LOOKAHEAD))
        for d, (_, _, _, _, d_ref, o_ref, _, _, order) in enumerate(dirs):
            c = order(j)
            dec = d_ref[c:c + 1, :]
            for p in pairs:
                qd, att, vbd, kv = cur[d][p]
                st = state[d][p]
                o = _dot(att, vbd) + _dot_nt(qd, st.astype(BF16))
                o_ref[c * CHUNK:(c + 1) * CHUNK, p * 2 * GLA_DV:(p + 1) * 2 * GLA_DV] = o.astype(o_ref.dtype)
                state[d][p] = st * dec[:, p * LANES:(p + 1) * LANES] + kv
    for d, (_, _, _, _, _, _, s_ref, _, _) in enumerate(dirs):
        for p in pairs:
            s_ref[p] = state[d][p]


def _gla(qf, kf, tf, qb, kb, tb, v, decf, decb, batch, seq):
    t = batch * seq
    nt = seq // TM_TOK
    fwd = lambda b, i: (b * nt + i, 0)
    bwd = lambda b, i: (b * nt + nt - 1 - i, 0)
    ks = lambda m: pl.BlockSpec((TM_TOK, GLA_KEY_WIDTH), m)
    vs = lambda m: pl.BlockSpec((TM_TOK, GLA_WIDTH), m)
    ds = lambda m: pl.BlockSpec((TM_TOK // CHUNK, GLA_KEY_WIDTH), m)
    o = jax.ShapeDtypeStruct((t, GLA_WIDTH), BF16)
    state = pltpu.VMEM((GLA_HEADS // 2, 2 * GLA_DV, LANES), F32)
    return pl.pallas_call(
        _gla_body,
        grid=(batch, nt),
        in_specs=[ks(fwd), ks(fwd), ks(fwd), vs(fwd), ds(fwd), ks(bwd), ks(bwd), ks(bwd), vs(bwd), ds(bwd)],
        out_specs=[vs(fwd), vs(bwd)],
        out_shape=[o, o],
        scratch_shapes=[state, state],
        compiler_params=pltpu.CompilerParams(dimension_semantics=("arbitrary", "arbitrary"),
                                             vmem_limit_bytes=VMEM_LIMIT),
        name="gla",
    )(qf, kf, tf, v, decf, qb, kb, tb, v, decb)


RADIX_ROWS = 512


def _radix4_body(z_ref, twc_ref, tws_ref, y_ref):
    z = [z_ref[s].astype(F32) for s in range(4)]
    re = [a[:, 0:F_WIDTH] for a in z]
    im = [a[:, F_WIDTH:] for a in z]
    ar, ai = re[0] + re[2], im[0] + im[2]
    br, bi = re[0] - re[2], im[0] - im[2]
    cr, ci = re[1] + re[3], im[1] + im[3]
    dr, di = re[1] - re[3], im[1] - im[3]
    y = [(ar + cr, ai + ci), (br + di, bi - dr), (ar - cr, ai - ci), (br - di, bi + dr)]
    y_ref[0, :, 0:F_WIDTH] = y[0][0].astype(BF16)
    y_ref[0, :, F_WIDTH:] = y[0][1].astype(BF16)
    for k1 in range(1, 4):
        c = jnp.concatenate([twc_ref[k1]] * (F_WIDTH // LANES), axis=1)
        s = jnp.concatenate([tws_ref[k1]] * (F_WIDTH // LANES), axis=1)
        yr, yi = y[k1]
        y_ref[k1, :, 0:F_WIDTH] = (yr * c + yi * s).astype(BF16)
        y_ref[k1, :, F_WIDTH:] = (yi * c - yr * s).astype(BF16)


def _radix4(u, batch):
    z = u.reshape(batch, 4, DFT_N, 2 * F_WIDTH)
    twc, tws = _twiddle_tables(4)
    twc, tws = jnp.asarray(twc), jnp.asarray(tws)
    nr = DFT_N // RADIX_ROWS
    blk = pl.BlockSpec((None, 4, RADIX_ROWS, 2 * F_WIDTH), lambda b, i: (b, 0, i, 0))
    tw = pl.BlockSpec((4, RADIX_ROWS, LANES), lambda b, i: (0, i, 0))
    y = pl.pallas_call(
        _radix4_body,
        grid=(batch, nr),
        in_specs=[blk, tw, tw],
        out_specs=blk,
        out_shape=jax.ShapeDtypeStruct((batch, 4, DFT_N, 2 * F_WIDTH), BF16),
        compiler_params=pltpu.CompilerParams(dimension_semantics=("arbitrary", "arbitrary"),
                                             vmem_limit_bytes=VMEM_LIMIT),
        name="radix4",
    )(z, twc, tws)
    return y


def _seqdft_body(cs_ref, edge_ref, mir_ref, ab_ref, o_ref, *, radix, scale):
    j = pl.program_id(1)
    first_row = lax.broadcasted_iota(jnp.int32, (DFT_ROWS, F_WIDTH), 0) == 0
    half = radix * DFT_ROWS
    rows = pl.ds(pl.multiple_of(j * DFT_ROWS, DFT_ROWS), DFT_ROWS)
    for k1 in range(radix):
        a = ab_ref[k1, :, 0:F_WIDTH]
        b = ab_ref[k1, :, F_WIDTH:]
        p = _dot(cs_ref[rows, 0:DFT_N], a)
        q = _dot(cs_ref[rows, DFT_N:], b)
        direct = (p + q) * scale
        edge = (_dot(edge_ref[:, 0:DFT_N], a) + _dot(edge_ref[:, DFT_N:], b)) * scale
        mirror = _dot(mir_ref[...], ((p - q) * scale).astype(BF16))
        mirror = jnp.where(first_row, jnp.where(j == 0, edge[1:2], edge[0:1]), mirror)
        for c in range(F_WIDTH // LANES):
            cols = slice(c * LANES, (c + 1) * LANES)
            o_ref[c, pl.ds(k1, DFT_ROWS, stride=radix), :] = direct[:, cols]
            o_ref[c, pl.ds(half + k1, DFT_ROWS, stride=radix), :] = mirror[:, cols]


def _seqdft_block(token_tile, tiles_per_batch, radix):
    b = token_tile // tiles_per_batch
    it = token_tile % tiles_per_batch
    ft = it // radix
    j = jnp.where(ft < 2, ft, 3 - ft)
    return ((b * 2 + j) * 2 + ft // 2) * radix + it % radix


def _seqdft(ab, cs, edge, mir, batch, radix):
    scale = float(1.0 / np.sqrt(radix * DFT_N))
    nj = DFT_N // (2 * DFT_ROWS)
    full = lambda a: pl.BlockSpec(a.shape, lambda b, j: (0,) * a.ndim)
    ab_bytes = radix * DFT_N * 2 * F_WIDTH * 2
    out_bytes = 2 * radix * DFT_ROWS * F_WIDTH * 4
    fixed_bytes = cs.size * 2 + 2 * (edge.size + mir.size) * 2 + 2 * out_bytes
    ab_buffers = 2 if fixed_bytes + 2 * ab_bytes <= VMEM_LIMIT - DFT_TEMP_BYTES else 1
    return pl.pallas_call(
        functools.partial(_seqdft_body, radix=radix, scale=scale),
        grid=(batch, nj),
        in_specs=[pl.BlockSpec(cs.shape, lambda b, j: (0, 0), pipeline_mode=pl.Buffered(1)), full(edge), full(mir),
                  pl.BlockSpec((None, radix, DFT_N, 2 * F_WIDTH), lambda b, j: (b, 0, 0, 0),
                               pipeline_mode=pl.Buffered(ab_buffers))],
        out_specs=pl.BlockSpec((F_WIDTH // LANES, 2 * radix * DFT_ROWS, LANES), lambda b, j: (0, b * nj + j, 0)),
        out_shape=jax.ShapeDtypeStruct((F_WIDTH // LANES, batch * radix * DFT_N, LANES), F32),
        compiler_params=pltpu.CompilerParams(dimension_semantics=("arbitrary", "arbitrary"),
                                             vmem_limit_bytes=VMEM_LIMIT),
        name="seqdft",
    )(cs, edge, mir, ab)


def _first_index(hit, rows):
    return jnp.min(jnp.where(hit, rows.astype(F32), 1e6), axis=0, keepdims=True).astype(jnp.int32)


def _postmix_body(of_ref, ob_ref, r_ref, fo_ref, x_ref, wo_ref, gg_ref, g2_ref, wr_ref,
                  h_ref, xn_ref, eidx_ref, cw_ref, rank_ref, cnt_ref, carry_ref):
    @pl.when(pl.program_id(0) == 0)
    def _():
        carry_ref[...] = jnp.zeros_like(carry_ref)

    o = of_ref[...].astype(F32) + ob_ref[...].astype(F32)
    r = r_ref[...].astype(F32)
    parts = []
    for hd in range(GLA_HEADS):
        sl = slice(hd * GLA_DV, (hd + 1) * GLA_DV)
        oh = o[:, sl]
        oh = oh * lax.rsqrt(jnp.mean(oh * oh, axis=-1, keepdims=True) + EPS)
        rh = r[:, sl]
        parts.append((oh * gg_ref[...] * (rh * jax.nn.sigmoid(rh))).astype(BF16))
    on = jnp.concatenate(parts, axis=1)
    fo = jnp.concatenate([fo_ref[c] for c in range(F_WIDTH // LANES)], axis=1)
    mixed = _dot(on, wo_ref[0:GLA_WIDTH, :]) + _dot(fo.astype(BF16), wo_ref[GLA_WIDTH:, :])
    h = x_ref[...] + mixed
    h_ref[...] = h
    xn = h * lax.rsqrt(jnp.mean(h * h, axis=-1, keepdims=True) + EPS) * g2_ref[...]
    _store_token_tiles(xn_ref, xn)
    logits = _dot_nt(wr_ref[...], xn.astype(BF16))

    sub8 = lax.broadcasted_iota(jnp.int32, (8, TM_TOK), 0)
    lg = jnp.where(sub8 < N_GROUPS, logits[0:8], -jnp.inf)
    gmax = jnp.max(lg, axis=0, keepdims=True)
    g_w = 1.0 / jnp.sum(jnp.exp(lg - gmax), axis=0, keepdims=True)
    g_sel = _first_index(lg == gmax, sub8)
    sel = logits[8:16]
    for g in range(1, N_GROUPS):
        sel = jnp.where(g_sel == g, logits[8 + 8 * g:16 + 8 * g], sel)
    m1 = jnp.max(sel, axis=0, keepdims=True)
    i1 = _first_index(sel == m1, sub8)
    sel2 = jnp.where(sub8 == i1, -jnp.inf, sel)
    m2 = jnp.max(sel2, axis=0, keepdims=True)
    i2 = _first_index(sel2 == m2, sub8)
    e21 = jnp.exp(m2 - m1)
    w1 = 1.0 / (1.0 + e21)
    w2 = e21 / (1.0 + e21)
    e1 = g_sel * EXPERTS_PER_GROUP + i1
    e2 = g_sel * EXPERTS_PER_GROUP + i2
    eidx_ref[...] = jnp.concatenate([e1, e2], axis=0)
    cw_ref[...] = jnp.concatenate([g_w * w1, g_w * w2], axis=0)

    sub = lax.broadcasted_iota(jnp.int32, (N_EXPERTS, TM_TOK), 0)
    oh1 = sub == e1
    oh2 = sub == e2
    oh1b = jnp.where(oh1, 1.0, 0.0).astype(BF16)
    oh2b = jnp.where(oh2, 1.0, 0.0).astype(BF16)
    before = (lax.broadcasted_iota(jnp.int32, (TM_TOK, TM_TOK), 0)
              < lax.broadcasted_iota(jnp.int32, (TM_TOK, TM_TOK), 1))
    before = jnp.where(before, 1.0, 0.0).astype(BF16)
    ones = jnp.ones((TM_TOK, LANES), BF16)
    oh12 = jnp.concatenate([oh1b, oh2b], axis=0)
    p12 = _dot(oh12, before)
    c12 = _dot(oh12, ones)
    p1, p2 = p12[0:N_EXPERTS], p12[N_EXPERTS:]
    c1, c2 = c12[0:N_EXPERTS], c12[N_EXPERTS:]
    carry = carry_ref[...]
    rep = TM_TOK // LANES
    base1 = jnp.concatenate([carry] * rep, axis=1)
    base2 = jnp.concatenate([carry + c1] * rep, axis=1)
    rk1 = jnp.sum(jnp.where(oh1, p1 + base1, 0.0), axis=0, keepdims=True)
    rk2 = jnp.sum(jnp.where(oh2, p2 + base2, 0.0), axis=0, keepdims=True)
    rank_ref[...] = jnp.concatenate([rk1, rk2], axis=0).astype(jnp.int32)
    carry = carry + c1 + c2
    carry_ref[...] = carry
    cnt_ref[...] = carry


def _postmix(of, ob, r, fo, fo_block, x2, wo, gg, g2, wr):
    t = x2.shape[0]
    nt = t // TM_TOK
    tok = lambda w: pl.BlockSpec((TM_TOK, w), lambda i: (i, 0))
    full = lambda a: pl.BlockSpec(a.shape, lambda i: (0,) * a.ndim)
    lane2 = pl.BlockSpec((2, TM_TOK), lambda i: (0, i))
    return pl.pallas_call(
        _postmix_body,
        grid=(nt,),
        in_specs=[tok(GLA_WIDTH), tok(GLA_WIDTH), tok(GLA_WIDTH),
                  pl.BlockSpec((F_WIDTH // LANES, TM_TOK, LANES), lambda i: (0, fo_block(i), 0)), tok(D_MODEL),
                  full(wo), full(gg), full(g2), full(wr)],
        out_specs=[tok(D_MODEL), pl.BlockSpec((TM_TOK * TOK_ROWS, LANES), lambda i: (i, 0)), lane2, lane2, lane2,
                   pl.BlockSpec((N_EXPERTS, LANES), lambda i: (0, 0))],
        out_shape=[jax.ShapeDtypeStruct((t, D_MODEL), F32), jax.ShapeDtypeStruct((t * TOK_ROWS, LANES), F32),
                   jax.ShapeDtypeStruct((2, t), jnp.int32), jax.ShapeDtypeStruct((2, t), F32),
                   jax.ShapeDtypeStruct((2, t), jnp.int32), jax.ShapeDtypeStruct((N_EXPERTS, LANES), F32)],
        scratch_shapes=[pltpu.VMEM((N_EXPERTS, LANES), F32)],
        compiler_params=pltpu.CompilerParams(dimension_semantics=("arbitrary",), vmem_limit_bytes=VMEM_LIMIT),
        name="postmix",
    )(of, ob, r, fo, x2, wo, gg, g2, wr)


def _store_token_tiles(ref, val):
    n = val.shape[0]
    for c in range(TOK_ROWS):
        ref[pl.ds(c, n, stride=TOK_ROWS), :] = val[:, c * LANES:(c + 1) * LANES]


def _load_token_tiles(ref, n):
    return jnp.concatenate([ref[pl.ds(c, n, stride=TOK_ROWS), :] for c in range(TOK_ROWS)], axis=1)


def _token_rows(tok):
    if isinstance(tok, int):
        return pl.ds(tok * TOK_ROWS, TOK_ROWS)
    return pl.ds(pl.multiple_of(tok * TOK_ROWS, TOK_ROWS), TOK_ROWS)


def _row_copy(src_hbm, dst_hbm, src_tok, dst_tok, sem):
    return pltpu.make_async_copy(src_hbm.at[_token_rows(src_tok)], dst_hbm.at[_token_rows(dst_tok)], sem)


def _dispatch_body(fill_ref, pos_ref, xn_ref, xs_hbm, zeros_ref, fill_sem, row_sem):
    i = pl.program_id(0)

    @pl.when(i == 0)
    def _():
        zeros_ref[...] = jnp.zeros_like(zeros_ref)
        tile_rows = TM_EXP * TOK_ROWS

        def fill(e):
            rows = pl.ds(pl.multiple_of(fill_ref[e] * tile_rows, tile_rows), tile_rows)
            return pltpu.make_async_copy(zeros_ref, xs_hbm.at[rows], fill_sem)

        for e in range(N_FILL):
            @pl.when(fill_ref[e] >= 0)
            def _():
                fill(e).start()
        for e in range(N_FILL):
            @pl.when(fill_ref[e] >= 0)
            def _():
                fill(e).wait()

    for j in range(TM_DISP):
        for s in range(2):
            _row_copy(xn_ref, xs_hbm, j, pos_ref[s, j], row_sem).start(priority=s)
    for j in range(TM_DISP):
        for s in range(2):
            _row_copy(xn_ref, xs_hbm, j, 0, row_sem).wait()


def _dispatch(xn, pos, fill_rows, rows_total):
    t = xn.shape[0] // TOK_ROWS
    grid_spec = pltpu.PrefetchScalarGridSpec(
        num_scalar_prefetch=1,
        grid=(t // TM_DISP,),
        in_specs=[pl.BlockSpec((2, TM_DISP), lambda i, fill: (0, i), memory_space=pltpu.SMEM),
                  pl.BlockSpec((TM_DISP * TOK_ROWS, LANES), lambda i, fill: (i, 0))],
        out_specs=pl.BlockSpec(memory_space=pl.ANY),
        scratch_shapes=[pltpu.VMEM((TM_EXP * TOK_ROWS, LANES), F32), pltpu.SemaphoreType.DMA(()),
                        pltpu.SemaphoreType.DMA(())],
    )
    return pl.pallas_call(
        _dispatch_body,
        grid_spec=grid_spec,
        out_shape=jax.ShapeDtypeStruct((rows_total * TOK_ROWS, LANES), F32),
        compiler_params=pltpu.CompilerParams(dimension_semantics=("arbitrary",), vmem_limit_bytes=VMEM_LIMIT),
        name="dispatch",
    )(fill_rows, pos, xn)


def _experts_body(te_ref, na_ref, xs_ref, wg_ref, wu_ref, wd_ref, ys_ref, wg_s, wu_s, wd_s):
    i = pl.program_id(0)
    active = i < na_ref[0]

    @pl.when(active & ((i == 0) | (te_ref[i] != te_ref[jnp.maximum(i - 1, 0)])))
    def _():
        wg_s[...] = wg_ref[...].astype(BF16)
        wu_s[...] = wu_ref[...].astype(BF16)
        wd_s[...] = wd_ref[...].astype(BF16)

    @pl.when(active)
    def _():
        x = _load_token_tiles(xs_ref, TM_EXP).astype(BF16)
        gate = _dot(x, wg_s[...])
        up = _dot(x, wu_s[...])
        hid = (gate * jax.nn.sigmoid(gate) * up).astype(BF16)
        _store_token_tiles(ys_ref, _dot(hid, wd_s[...]))

    @pl.when(jnp.logical_not(active))
    def _():
        ys_ref[...] = jnp.zeros_like(ys_ref)


def _experts(xs, tile_expert, n_active, wg, wu, wd):
    n_tiles = xs.shape[0] // (TM_EXP * TOK_ROWS)
    row = lambda i, te, na: (i, 0)
    row_in = lambda i, te, na: (jnp.minimum(i, na[0] - 1), 0)
    wsel = lambda i, te, na: (te[jnp.minimum(i, na[0] - 1)], 0, 0)
    grid_spec = pltpu.PrefetchScalarGridSpec(
        num_scalar_prefetch=2,
        grid=(n_tiles,),
        in_specs=[pl.BlockSpec((TM_EXP * TOK_ROWS, LANES), row_in),
                  pl.BlockSpec((None, D_MODEL, D_EXPERT), wsel),
                  pl.BlockSpec((None, D_MODEL, D_EXPERT), wsel),
                  pl.BlockSpec((None, D_EXPERT, D_MODEL), wsel)],
        out_specs=pl.BlockSpec((TM_EXP * TOK_ROWS, LANES), row),
        scratch_shapes=[pltpu.VMEM((D_MODEL, D_EXPERT), BF16), pltpu.VMEM((D_MODEL, D_EXPERT), BF16),
                        pltpu.VMEM((D_EXPERT, D_MODEL), BF16)],
    )
    return pl.pallas_call(
        _experts_body,
        grid_spec=grid_spec,
        out_shape=jax.ShapeDtypeStruct(xs.shape, F32),
        compiler_params=pltpu.CompilerParams(dimension_semantics=("arbitrary",), vmem_limit_bytes=VMEM_LIMIT),
        name="experts",
    )(tile_expert, n_active, xs, wg, wu, wd)


def _combine_body(pos_ref, nxt_ref, ys_hbm, cw_ref, h_ref, gf_ref, y_ref, buf_ref, sem):
    i = pl.program_id(0)
    slot = lax.rem(i, 2)

    def row_gather(p_ref, sl, s, j):
        return pltpu.make_async_copy(ys_hbm.at[_token_rows(p_ref[s, j])], buf_ref.at[sl, s, _token_rows(j)],
                                     sem.at[sl])

    def issue_tile(p_ref, sl):
        for j in range(TM_COMB):
            for s in range(2):
                row_gather(p_ref, sl, s, j).start(priority=s)

    @pl.when(i == 0)
    def _():
        issue_tile(pos_ref, slot)

    @pl.when(i + 1 < pl.num_programs(0))
    def _():
        issue_tile(nxt_ref, 1 - slot)

    for j in range(TM_COMB):
        for s in range(2):
            pltpu.make_async_copy(ys_hbm.at[_token_rows(0)], buf_ref.at[slot, s, _token_rows(j)], sem.at[slot]).wait()

    cw = cw_ref[...]
    y0 = _load_token_tiles(buf_ref.at[slot, 0], TM_COMB)
    y1 = _load_token_tiles(buf_ref.at[slot, 1], TM_COMB)
    h = h_ref[...] + cw[:, 0:1] * y0 + cw[:, 1:2] * y1
    y_ref[...] = h * lax.rsqrt(jnp.mean(h * h, axis=-1, keepdims=True) + EPS) * gf_ref[...]


def _combine(ys, pos, cw_t, h, gf):
    t = h.shape[0]
    n = t // TM_COMB
    tok = pl.BlockSpec((TM_COMB, D_MODEL), lambda i: (i, 0))
    return pl.pallas_call(
        _combine_body,
        grid=(n,),
        in_specs=[pl.BlockSpec((2, TM_COMB), lambda i: (0, i), memory_space=pltpu.SMEM),
                  pl.BlockSpec((2, TM_COMB), lambda i: (0, jnp.minimum(i + 1, n - 1)), memory_space=pltpu.SMEM),
                  pl.BlockSpec(memory_space=pl.ANY),
                  pl.BlockSpec((TM_COMB, 2), lambda i: (i, 0)),
                  tok,
                  pl.BlockSpec((1, D_MODEL), lambda i: (0, 0))],
        out_specs=tok,
        out_shape=jax.ShapeDtypeStruct((t, D_MODEL), F32),
        scratch_shapes=[pltpu.VMEM((2, 2, TM_COMB * TOK_ROWS, LANES), F32), pltpu.SemaphoreType.DMA((2,))],
        compiler_params=pltpu.CompilerParams(dimension_semantics=("arbitrary",), vmem_limit_bytes=VMEM_LIMIT),
        name="combine",
    )(pos, pos, ys, cw_t, h, gf)


def _prepare_params(norm1_g, w_in, w_gk2_f, b_gk_f, w_gk2_b, b_gk_b, gla_norm_g, w_out,
                    norm2_g, w_group, w_expert, w_gate, w_up, w_down, norm_f_g):
    w = w_in[0] * norm1_g[0][:, None]
    gate_lo = 2 * GLA_KEY_WIDTH + 2 * GLA_WIDTH
    gate_hi = gate_lo + 2 * GATE_RANK
    w_in_r = jnp.concatenate([w[:, :gate_lo], w[:, gate_hi:], w[:, gate_lo:gate_hi],
                              jnp.zeros((D_MODEL, GATE_COLS - 2 * GATE_RANK), F32)], axis=1).astype(BF16)
    zk = jnp.zeros((GATE_RANK, GLA_KEY_WIDTH), F32)
    wg = jnp.concatenate([jnp.concatenate([w_gk2_f[0], zk], axis=1), jnp.concatenate([zk, w_gk2_b[0]], axis=1),
                          jnp.zeros((GATE_COLS - 2 * GATE_RANK, 2 * GLA_KEY_WIDTH), F32)], axis=0).astype(BF16)
    bg = jnp.concatenate([b_gk_f[0], b_gk_b[0]])[None, :]
    wr = jnp.concatenate([w_group[0].T, jnp.zeros((8 - N_GROUPS, D_MODEL), F32), w_expert[0].T,
                          jnp.zeros((ROUTER_ROWS - 8 - N_EXPERTS, D_MODEL), F32)], axis=0).astype(BF16)
    return dict(
        w_in_r=w_in_r, wg=wg, bg=bg,
        gg=gla_norm_g[0][None, :], wo=w_out[0].astype(BF16), g2=norm2_g[0][None, :], wr=wr,
        w_gate=w_gate[0], w_up=w_up[0], w_down=w_down[0],
        gf=norm_f_g[None, :],
        trif=jnp.asarray(_TRI_F).astype(BF16), trib=jnp.asarray(_TRI_B).astype(BF16),
        chan_dft=jnp.asarray(_CHAN_DFT).astype(BF16), seq_dft=jnp.asarray(_SEQ_DFT_LOWER).astype(BF16),
        seq_edge=jnp.asarray(_SEQ_DFT_EDGE).astype(BF16), seq_mirror=jnp.asarray(_SEQ_DFT_MIRROR).astype(BF16))


def _encoder(x, p):
    batch, seq, _ = x.shape
    t = batch * seq
    assert seq % DFT_N == 0 and seq // DFT_N in (1, 4), "sequence DFT supports seq = 2048 or 8192"
    radix = seq // DFT_N
    x2 = x.reshape(t, D_MODEL)
    qf, kf, tf, qb, kb, tb, v, r, u, decf, decb = _inproj(x2, p["w_in_r"], p["wg"], p["bg"],
                                                          p["trif"], p["trib"], p["chan_dft"])
    of, ob = _gla(qf, kf, tf, qb, kb, tb, v, decf, decb, batch, seq)
    if radix == 1:
        ab = u.reshape(batch, 1, DFT_N, 2 * F_WIDTH)
    else:
        ab = _radix4(u, batch)
    fo = _seqdft(ab, p["seq_dft"], p["seq_edge"], p["seq_mirror"], batch, radix)
    fo_block = functools.partial(_seqdft_block, tiles_per_batch=seq // TM_TOK, radix=radix)
    h, xn, eidx, cw, rank, cnt = _postmix(of, ob, r, fo, fo_block, x2, p["wo"], p["gg"], p["g2"], p["wr"])

    counts = cnt[:, 0].astype(jnp.int32)
    tiles = (counts + TM_EXP - 1) // TM_EXP
    tile_end = jnp.cumsum(tiles)
    tile_start = tile_end - tiles
    experts = jnp.arange(N_EXPERTS, dtype=jnp.int32)
    seg_row = jnp.sum(jnp.where(eidx[:, :, None] == experts, tile_start * TM_EXP, 0), axis=-1)
    pos = seg_row + rank
    n_tiles = 2 * t // TM_EXP + N_EXPERTS
    tile_ids = jnp.arange(n_tiles, dtype=jnp.int32)
    tile_expert = jnp.minimum(jnp.sum((tile_end[None, :] <= tile_ids[:, None]).astype(jnp.int32), axis=1),
                              N_EXPERTS - 1)
    n_active = tile_end[-1:].astype(jnp.int32)
    tail = n_active + jnp.arange(N_EXPERTS, dtype=jnp.int32)
    fill_tiles = jnp.concatenate([jnp.where(tiles > 0, tile_end - 1, -1),
                                  jnp.where(tail < n_tiles, tail, -1)]).astype(jnp.int32)

    xs = _dispatch(xn, pos, fill_tiles, n_tiles * TM_EXP)
    ys = _experts(xs, tile_expert, n_active, p["w_gate"], p["w_up"], p["w_down"])
    y = _combine(ys, pos, cw.T, h, p["gf"])
    return y.reshape(batch, seq, D_MODEL)


def kernel(x_prompt, x_sample, norm1_g, w_in, w_gk2_f, b_gk_f, w_gk2_b, b_gk_b, gla_norm_g, w_out, norm2_g,
           w_group, w_expert, w_gate, w_up, w_down, norm_f_g):
    p = _prepare_params(norm1_g, w_in, w_gk2_f, b_gk_f, w_gk2_b, b_gk_b, gla_norm_g, w_out,
                        norm2_g, w_group, w_expert, w_gate, w_up, w_down, norm_f_g)
    return (_encoder(x_prompt, p), _encoder(x_sample, p))
```
